```python
import math
import jax
import jax.numpy as jnp
from jax import lax
import numpy as np

D_MODEL = 1024
BATCH = 8
SEQ = 2048
DEPTH = 4
DEC_BATCH = 32
DEC_SEQ = 8
PAST_LEN = 8192
PAGE_SIZE = 128

D_MIX = D_MODEL
GROUP_W = D_MIX // 4
HEAD_DIM = 64
SSM_HEADS = GROUP_W // HEAD_DIM
SSM_GROUPS = 2
SSM_STATE = 128
SSM_CONV = 4
SSM_CHUNK = 128
SSM_XBC = GROUP_W + 2 * SSM_GROUPS * SSM_STATE
MOBA_HEADS = GROUP_W // HEAD_DIM
MOBA_BLOCK = 256
MOBA_TOPK = 3
DIFF_HEADS = GROUP_W // HEAD_DIM
DIFF_DH = GROUP_W // (2 * DIFF_HEADS)
RWKV_HEADS = GROUP_W // HEAD_DIM
RWKV_DECAY_LORA = 64
RWKV_A_LORA = 64
RWKV_GATE_LORA = 128
RWKV_PROJ = 3 * GROUP_W + RWKV_DECAY_LORA + RWKV_A_LORA + RWKV_GATE_LORA
RWKV_LN_EPS = 64e-5
ROPE_THETA = 500000.0
ROPE_FRACTION = 4
Q_BLOCK = 128
D_FF = -(-8 * D_MODEL // (3 * 256)) * 256
PROJ_A = GROUP_W + SSM_XBC + SSM_HEADS
PROJ_B = 3 * GROUP_W
PROJ_C = 3 * GROUP_W
PROJ_D = RWKV_PROJ
IN_PROJ = PROJ_A + PROJ_B + PROJ_C + PROJ_D
N_MOD = 6
RMS_EPS = 1e-6

kernel_name = 'hybrid_ssd_moba_diff_rwkv7_decode_step'


def rmsnorm(x, w):
    xf = x.astype(jnp.float32)
    y = xf * lax.rsqrt(jnp.mean(xf * xf, axis=-1, keepdims=True) + RMS_EPS)
    return (y * w).astype(x.dtype)


def partial_rope(x, pos):
    rot = x.shape[-1] // ROPE_FRACTION
    half = rot // 2
    inv_freq = ROPE_THETA ** (-jnp.arange(half, dtype=jnp.float32) / half)
    ang = pos.astype(jnp.float32)[:, None] * inv_freq[None, :]
    cos = jnp.cos(ang)[None, :, None, :]
    sin = jnp.sin(ang)[None, :, None, :]
    x1 = x[..., :half].astype(jnp.float32)
    x2 = x[..., half:rot].astype(jnp.float32)
    rotated = jnp.concatenate([x1 * cos - x2 * sin, x2 * cos + x1 * sin], axis=-1).astype(x.dtype)
    return jnp.concatenate([rotated, x[..., rot:]], axis=-1)


def causal_depthwise_conv(u, prev, w, b):
    t = u.shape[1]
    full = jnp.concatenate([prev.astype(u.dtype), u], axis=1)
    out = b + sum(full[:, i:i + t] * w[i] for i in range(SSM_CONV))
    return out, full[:, full.shape[1] - (SSM_CONV - 1):]


def segsum(a):
    cs = jnp.cumsum(a, axis=-1)
    diff = cs[..., :, None] - cs[..., None, :]
    n = a.shape[-1]
    return jnp.where(jnp.tril(jnp.ones((n, n), dtype=bool)), diff, -jnp.inf)


def ssd_scan(x, da, bm, cm, s0):
    b, t, g, j, p = x.shape
    n = bm.shape[-1]
    q = math.gcd(t, SSM_CHUNK)
    c = t // q
    xc = x.reshape(b, c, q, g, j, p)
    bc = bm.reshape(b, c, q, g, n)
    cc = cm.reshape(b, c, q, g, n)
    ac = jnp.transpose(da.reshape(b, c, q, g, j), (0, 3, 4, 1, 2))
    a_cs = jnp.cumsum(ac, axis=-1)
    decay_in = jnp.exp(segsum(ac))
    y_diag = jnp.einsum('bclgn,bcsgn,bgjcls,bcsgjp->bclgjp', cc, bc, decay_in, xc)
    decay_to_end = jnp.exp(a_cs[..., -1:] - a_cs)
    chunk_states = jnp.einsum('bclgn,bgjcl,bclgjp->bcgjpn', bc, decay_to_end, xc)
    chunk_states = jnp.concatenate([s0[:, None], chunk_states], axis=1)
    chunk_decay = jnp.exp(segsum(jnp.pad(a_cs[..., -1], ((0, 0), (0, 0), (0, 0), (1, 0)))))
    states = jnp.einsum('bgjzc,bcgjpn->bzgjpn', chunk_decay, chunk_states)
    y_off = jnp.einsum('bclgn,bcgjpn,bgjcl->bclgjp', cc, states[:, :-1], jnp.exp(a_cs))
    return (y_diag + y_off).reshape(b, t, g, j, p), states[:, -1]


def mamba2_mixer(u, pl, conv_prev, ssm_prev):
    b, t, _ = u.shape
    jh = SSM_HEADS // SSM_GROUPS
    z = u[..., :GROUP_W]
    xbc = u[..., GROUP_W:GROUP_W + SSM_XBC]
    dt_raw = u[..., GROUP_W + SSM_XBC:]
    if conv_prev is None:
        conv_prev = jnp.zeros((b, SSM_CONV - 1, SSM_XBC), u.dtype)
    if ssm_prev is None:
        ssm_prev = jnp.zeros((b, SSM_HEADS, HEAD_DIM, SSM_STATE), jnp.float32)
    xbc, conv_new = causal_depthwise_conv(xbc, conv_prev, pl['conv_w'], pl['conv_b'])
    xbc = jax.nn.silu(xbc).astype(jnp.float32)
    xs = xbc[..., :GROUP_W].reshape(b, t, SSM_GROUPS, jh, HEAD_DIM)
    bm = xbc[..., GROUP_W:GROUP_W + SSM_GROUPS * SSM_STATE].reshape(b, t, SSM_GROUPS, SSM_STATE)
    cm = xbc[..., GROUP_W + SSM_GROUPS * SSM_STATE:].reshape(b, t, SSM_GROUPS, SSM_STATE)
    dt = jax.nn.softplus(dt_raw.astype(jnp.float32) + pl['dt_bias']).reshape(b, t, SSM_GROUPS, jh)
    a = -jnp.exp(pl['a_log'].astype(jnp.float32)).reshape(SSM_GROUPS, jh)
    s0 = ssm_prev.astype(jnp.float32).reshape(b, SSM_GROUPS, jh, HEAD_DIM, SSM_STATE)
    y, s_final = ssd_scan(xs * dt[..., None], dt * a, bm, cm, s0)
    y = y + pl['d_skip'].reshape(SSM_GROUPS, jh)[:, :, None] * xs
    y = y.reshape(b, t, GROUP_W) * jax.nn.silu(z.astype(jnp.float32))
    y = rmsnorm(y, pl['ssm_norm_w'])
    return y.astype(u.dtype), conv_new, s_final.reshape(b, SSM_HEADS, HEAD_DIM, SSM_STATE)


def moba_attention(q, k, v, q_pos):
    b, t, h, d = q.shape
    pad = (-k.shape[1]) % MOBA_BLOCK
    k = jnp.pad(k, ((0, 0), (0, pad), (0, 0), (0, 0)))
    v = jnp.pad(v, ((0, 0), (0, pad), (0, 0), (0, 0)))
    nb = k.shape[1] // MOBA_BLOCK
    topk = min(MOBA_TOPK, nb)
    qb = math.gcd(t, Q_BLOCK)
    scale = d ** -0.5
    head_idx = jnp.arange(h)[None, :, None]
    offs = jnp.arange(MOBA_BLOCK)

    def per_sequence(args):
        qs, ks, vs = args
        k_blocks = jnp.transpose(ks.reshape(nb, MOBA_BLOCK, h, d), (2, 0, 1, 3))
        v_blocks = jnp.transpose(vs.reshape(nb, MOBA_BLOCK, h, d), (2, 0, 1, 3))
        k_mean = jnp.mean(k_blocks.astype(jnp.float32), axis=2)

        def per_chunk(cargs):
            qc, pc = cargs
            own = pc // MOBA_BLOCK
            gate = jnp.einsum('qhd,hnd->qhn', qc.astype(jnp.float32), k_mean)
            gate = jnp.where(jnp.arange(nb)[None, None, :] < own[:, None, None], gate, -jnp.inf)
            _, top = lax.top_k(gate, topk)
            blk = jnp.concatenate([top, jnp.broadcast_to(own[:, None, None], (qb, h, 1))], axis=-1)
            blk_ok = jnp.concatenate([
                jnp.broadcast_to(jnp.arange(topk)[None, None, :] < own[:, None, None], (qb, h, topk)),
                jnp.ones((qb, h, 1), dtype=bool)], axis=-1)
            kg = k_blocks[head_idx, blk]
            vg = v_blocks[head_idx, blk]
            k_pos = blk[..., None] * MOBA_BLOCK + offs
            ok = blk_ok[..., None] & (k_pos <= pc[:, None, None, None])
            s = jnp.einsum('qhd,qhskd->qhsk', qc, kg).astype(jnp.float32) * scale
            s = jnp.where(ok, s, -jnp.inf).reshape(qb, h, -1)
            p = jax.nn.softmax(s, axis=-1).reshape(ok.shape)
            return jnp.einsum('qhsk,qhskd->qhd', p.astype(vs.dtype), vg)

        out = lax.map(per_chunk, (qs.reshape(t // qb, qb, h, d), q_pos.reshape(t // qb, qb)))
        return out.reshape(t, h, d)

    return lax.map(per_sequence, (q, k, v))


def moba_mixer(u, pos, k_past, v_past):
    b, t, _ = u.shape
    q, k, v = [u[..., i * GROUP_W:(i + 1) * GROUP_W].reshape(b, t, MOBA_HEADS, HEAD_DIM) for i in range(3)]
    q = partial_rope(q, pos)
    k = partial_rope(k, pos)
    k_all = k if k_past is None else jnp.concatenate([k_past.astype(k.dtype), k], axis=1)
    v_all = v if v_past is None else jnp.concatenate([v_past.astype(v.dtype), v], axis=1)
    o = moba_attention(q, k_all, v_all, pos)
    return o.reshape(b, t, GROUP_W), k, v


def to_qblocks(a, qb):
    b, t = a.shape[:2]
    return jnp.moveaxis(a.reshape((b, t // qb, qb) + a.shape[2:]), 1, 0)


def diff_attention(q1, q2, k1, k2, v, q_pos, lam):
    b, t, h, d = q1.shape
    qb = math.gcd(t, Q_BLOCK)
    k_pos = jnp.arange(k1.shape[1])
    scale = d ** -0.5

    def block(args):
        a1, a2, pb = args
        visible = k_pos[None, None, None, :] <= pb[None, None, :, None]

        def attn_map(qq, kk):
            s = jnp.einsum('bqhd,bkhd->bhqk', qq, kk).astype(jnp.float32) * scale
            return jax.nn.softmax(jnp.where(visible, s, -jnp.inf), axis=-1)

        weights = attn_map(a1, k1) - lam * attn_map(a2, k2)
        return jnp.einsum('bhqk,bkhe->bqhe', weights.astype(v.dtype), v)

    out = lax.map(block, (to_qblocks(q1, qb), to_qblocks(q2, qb), q_pos.reshape(t // qb, qb)))
    return jnp.moveaxis(out, 0, 1).reshape(b, t, h, v.shape[-1])


def diff_mixer(u, pos, pl, lam_init, k_past, v_past):
    b, t, _ = u.shape
    q = u[..., :GROUP_W].reshape(b, t, DIFF_HEADS, 2, DIFF_DH)
    k = u[..., GROUP_W:2 * GROUP_W].reshape(b, t, DIFF_HEADS, 2, DIFF_DH)
    v = u[..., 2 * GROUP_W:].reshape(b, t, DIFF_HEADS, 2 * DIFF_DH)
    q1 = partial_rope(q[..., 0, :], pos)
    q2 = partial_rope(q[..., 1, :], pos)
    k_rows = jnp.concatenate([partial_rope(k[..., 0, :], pos), partial_rope(k[..., 1, :], pos)], axis=-1)
    k_all = k_rows if k_past is None else jnp.concatenate([k_past.astype(k_rows.dtype), k_rows], axis=1)
    v_all = v if v_past is None else jnp.concatenate([v_past.astype(v.dtype), v], axis=1)
    lam = (jnp.exp(jnp.sum(pl['lambda_q1'] * pl['lambda_k1']).astype(jnp.float32))
           - jnp.exp(jnp.sum(pl['lambda_q2'] * pl['lambda_k2']).astype(jnp.float32)) + lam_init)
    o = diff_attention(q1, q2, k_all[..., :DIFF_DH], k_all[..., DIFF_DH:], v_all, pos, lam)
    o = rmsnorm(o, pl['subln_w']) * (1.0 - lam_init)
    return o.reshape(b, t, GROUP_W), k_rows, v


def wkv7_scan(r, decay, k, v, kk, a, s0):
    xs = tuple(jnp.moveaxis(z, 1, 0) for z in (r, decay, k, v, kk, a))

    def step(s, inp):
        rt, wt, kt, vt, kkt, at = inp
        sa = jnp.einsum('bhij,bhj->bhi', s, -kkt)
        s = s * wt[:, :, None, :] + sa[..., None] * (kkt * at)[:, :, None, :] + vt[..., None] * kt[:, :, None, :]
        return s, jnp.einsum('bhij,bhj->bhi', s, rt)

    s_final, ys = lax.scan(step, s0, xs)
    return jnp.moveaxis(ys, 0, 1), s_final


def rwkv7_mixer(u, pl, shift_prev, wkv_prev):
    b, t, _ = u.shape
    if shift_prev is None:
        shift_prev = jnp.zeros((b, RWKV_PROJ), u.dtype)
    if wkv_prev is None:
        wkv_prev = jnp.zeros((b, RWKV_HEADS, HEAD_DIM, HEAD_DIM), jnp.float32)
    u_prev = jnp.concatenate([shift_prev[:, None].astype(u.dtype), u[:, :-1]], axis=1)
    us = (u + pl['mu_shift'] * (u_prev - u)).astype(jnp.float32)
    o1, o2, o3 = GROUP_W, 2 * GROUP_W, 3 * GROUP_W
    o4 = o3 + RWKV_DECAY_LORA
    o5 = o4 + RWKV_A_LORA
    r, k, v = us[..., :o1], us[..., o1:o2], us[..., o2:o3]
    wd, ad, gd = us[..., o3:o4], us[..., o4:o5], us[..., o5:]
    w = -jax.nn.softplus(-(pl['w0'] + jnp.tanh(wd) @ pl['w2'])) - 0.5
    a = jax.nn.sigmoid(pl['a0'] + ad @ pl['a2'])
    g = jax.nn.sigmoid(gd) @ pl['g2']

    def heads(z):
        return z.reshape(b, t, RWKV_HEADS, HEAD_DIM)

    kk = heads(k * pl['k_k'])
    kk = kk / jnp.maximum(jnp.sqrt(jnp.sum(kk * kk, axis=-1, keepdims=True)), 1e-12)
    k = k * (1.0 + (a - 1.0) * pl['k_a'])
    r, k, v, a, decay = heads(r), heads(k), heads(v), heads(a), heads(jnp.exp(-jnp.exp(w)))
    y, s_final = wkv7_scan(r, decay, k, v, kk, a, wkv_prev.astype(jnp.float32))
    mu = jnp.mean(y, axis=-1, keepdims=True)
    var = jnp.mean((y - mu) ** 2, axis=-1, keepdims=True)
    y = ((y - mu) * lax.rsqrt(var + RWKV_LN_EPS)).reshape(b, t, GROUP_W) * pl['ln_x_w'] + pl['ln_x_b']
    bonus = jnp.sum(r * k * pl['r_k'], axis=-1, keepdims=True) * v
    y = (y + bonus.reshape(b, t, GROUP_W)) * g
    return y.astype(u.dtype), u[:, -1], s_final


def decoder_layer(x, c, pos, pl, lam_init, past):
    if past is None:
        past = (None,) * 8
    mk_past, mv_past, dk_past, dv_past, ssm_past, conv_past, wkv_past, shift_past = past
    mod = jax.nn.silu(c) @ pl['w_ada'] + pl['b_ada']
    shift_a, scale_a, gate_a, shift_f, scale_f, gate_f = jnp.split(mod[:, None, :], N_MOD, axis=-1)
    h = rmsnorm(x, pl['norm_mix_pre']) * (1.0 + scale_a) + shift_a
    proj = h @ pl['w_in']
    o1 = PROJ_A
    o2 = o1 + PROJ_B
    o3 = o2 + PROJ_C
    y_a, conv_new, ssm_new = mamba2_mixer(proj[..., :o1], pl, conv_past, ssm_past)
    y_b, mk_new, mv_new = moba_mixer(proj[..., o1:o2], pos, mk_past, mv_past)
    y_c, dk_new, dv_new = diff_mixer(proj[..., o2:o3], pos, pl, lam_init, dk_past, dv_past)
    y_d, shift_new, wkv_new = rwkv7_mixer(proj[..., o3:], pl, shift_past, wkv_past)
    mixed = jnp.concatenate([y_a, y_b, y_c, y_d], axis=-1) @ pl['w_out']
    x = x + (1.0 + gate_a) * rmsnorm(mixed, pl['norm_mix_post'])
    h = rmsnorm(x, pl['norm_ffn_pre']) * (1.0 + scale_f) + shift_f
    ff = (jax.nn.silu(h @ pl['w_gate']) * (h @ pl['w_up'])) @ pl['w_down']
    x = x + (1.0 + gate_f) * rmsnorm(ff, pl['norm_ffn_post'])
    return x, (mk_new, mv_new, dk_new, dv_new, ssm_new, conv_new, wkv_new, shift_new)


def gather_pages(pool, page_table):
    g = pool[page_table]
    return g.reshape((g.shape[0], g.shape[1] * g.shape[2]) + g.shape[3:])


def setup_inputs(seed: int = 0) -> dict:
    key = jax.random.key(seed)
    keys = list(jax.random.split(key, 64))
    f32 = jnp.float32

    def nrm(shape, scale=1.0):
        return scale * jax.random.normal(keys.pop(), shape, f32)

    def unif(shape, lo, hi):
        return jax.random.uniform(keys.pop(), shape, f32, lo, hi)

    n_pages = PAST_LEN // PAGE_SIZE
    n_used = DEC_BATCH * n_pages
    n_pool = n_used + n_used // 4
    page_table = jax.random.permutation(keys.pop(), n_pool)[:n_used].reshape(DEC_BATCH, n_pages).astype(jnp.int32)
    dt0 = jnp.exp(unif((DEPTH, SSM_HEADS), math.log(1e-3), math.log(1e-1)))
    return {
        'x_prompt': nrm((BATCH, SEQ, D_MODEL)),
        'x_sample': nrm((DEC_BATCH, DEC_SEQ, D_MODEL)),
        'c_prompt': nrm((BATCH, D_MODEL)),
        'c_sample': nrm((DEC_BATCH, D_MODEL)),
        'cache_moba_k': nrm((DEPTH, n_pool, PAGE_SIZE, MOBA_HEADS, HEAD_DIM)),
        'cache_moba_v': nrm((DEPTH, n_pool, PAGE_SIZE, MOBA_HEADS, HEAD_DIM)),
        'cache_diff_k': nrm((DEPTH, n_pool, PAGE_SIZE, DIFF_HEADS, 2 * DIFF_DH)),
        'cache_diff_v': nrm((DEPTH, n_pool, PAGE_SIZE, DIFF_HEADS, 2 * DIFF_DH)),
        'page_table': page_table,
        'state_ssm': nrm((DEPTH, DEC_BATCH, SSM_HEADS, HEAD_DIM, SSM_STATE), 0.1),
        'state_conv': nrm((DEPTH, DEC_BATCH, SSM_CONV - 1, SSM_XBC)),
        'state_wkv': nrm((DEPTH, DEC_BATCH, RWKV_HEADS, HEAD_DIM, HEAD_DIM), 0.5),
        'state_shift': nrm((DEPTH, DEC_BATCH, RWKV_PROJ)),
        'w_ada': nrm((DEPTH, D_MODEL, N_MOD * D_MODEL), 0.1 * D_MODEL ** -0.5),
        'b_ada': nrm((DEPTH, N_MOD * D_MODEL), 0.02),
        'norm_mix_pre': 1.0 + nrm((DEPTH, D_MODEL), 0.05),
        'norm_mix_post': 1.0 + nrm((DEPTH, D_MODEL), 0.05),
        'norm_ffn_pre': 1.0 + nrm((DEPTH, D_MODEL), 0.05),
        'norm_ffn_post': 1.0 + nrm((DEPTH, D_MODEL), 0.05),
        'w_in': nrm((DEPTH, D_MODEL, IN_PROJ), D_MODEL ** -0.5),
        'w_out': nrm((DEPTH, D_MIX, D_MODEL), D_MIX ** -0.5),
        'conv_w': nrm((DEPTH, SSM_CONV, SSM_XBC), SSM_CONV ** -0.5),
        'conv_b': nrm((DEPTH, SSM_XBC), 0.02),
        'dt_bias': dt0 + jnp.log(-jnp.expm1(-dt0)),
        'a_log': jnp.log(unif((DEPTH, SSM_HEADS), 1.0, 16.0)),
        'd_skip': 1.0 + nrm((DEPTH, SSM_HEADS), 0.1),
        'ssm_norm_w': 1.0 + nrm((DEPTH, GROUP_W), 0.05),
        'lambda_q1': nrm((DEPTH, DIFF_DH), 0.1),
        'lambda_k1': nrm((DEPTH, DIFF_DH), 0.1),
        'lambda_q2': nrm((DEPTH, DIFF_DH), 0.1),
        'lambda_k2': nrm((DEPTH, DIFF_DH), 0.1),
        'subln_w': 1.0 + nrm((DEPTH, 2 * DIFF_DH), 0.05),
        'mu_shift': unif((DEPTH, RWKV_PROJ), 0.0, 1.0),
        'w0': unif((DEPTH, GROUP_W), -6.5, -1.5),
        'w2': nrm((DEPTH, RWKV_DECAY_LORA, GROUP_W), 0.5 * RWKV_DECAY_LORA ** -0.5),
        'a0': nrm((DEPTH, GROUP_W), 0.1),
        'a2': nrm((DEPTH, RWKV_A_LORA, GROUP_W), 0.5 * RWKV_A_LORA ** -0.5),
        'g2': nrm((DEPTH, RWKV_GATE_LORA, GROUP_W), RWKV_GATE_LORA ** -0.5),
        'k_k': 0.85 + nrm((DEPTH, GROUP_W), 0.1),
        'k_a': 1.0 + nrm((DEPTH, GROUP_W), 0.1),
        'r_k': nrm((DEPTH, RWKV_HEADS, HEAD_DIM), 0.1),
        'ln_x_w': 1.0 + nrm((DEPTH, GROUP_W), 0.05),
        'ln_x_b': nrm((DEPTH, GROUP_W), 0.02),
        'w_gate': nrm((DEPTH, D_MODEL, D_FF), D_MODEL ** -0.5),
        'w_up': nrm((DEPTH, D_MODEL, D_FF), D_MODEL ** -0.5),
        'w_down': nrm((DEPTH, D_FF, D_MODEL), D_FF ** -0.5),
    }


def reference(x_prompt, x_sample, c_prompt, c_sample, cache_moba_k, cache_moba_v, cache_diff_k, cache_diff_v,
              page_table, state_ssm, state_conv, state_wkv, state_shift, w_ada, b_ada, norm_mix_pre,
              norm_mix_post, norm_ffn_pre, norm_ffn_post, w_in, w_out, conv_w, conv_b, dt_bias, a_log, d_skip,
              ssm_norm_w, lambda_q1, lambda_k1, lambda_q2, lambda_k2, subln_w, mu_shift, w0, w2, a0, a2, g2,
              k_k, k_a, r_k, ln_x_w, ln_x_b, w_gate, w_up, w_down):
    params = dict(w_ada=w_ada, b_ada=b_ada, norm_mix_pre=norm_mix_pre, norm_mix_post=norm_mix_post,
                  norm_ffn_pre=norm_ffn_pre, norm_ffn_post=norm_ffn_post, w_in=w_in, w_out=w_out,
                  conv_w=conv_w, conv_b=conv_b, dt_bias=dt_bias, a_log=a_log, d_skip=d_skip,
                  ssm_norm_w=ssm_norm_w, lambda_q1=lambda_q1, lambda_k1=lambda_k1, lambda_q2=lambda_q2,
                  lambda_k2=lambda_k2, subln_w=subln_w, mu_shift=mu_shift, w0=w0, w2=w2, a0=a0, a2=a2,
                  g2=g2, k_k=k_k, k_a=k_a, r_k=r_k, ln_x_w=ln_x_w, ln_x_b=ln_x_b, w_gate=w_gate,
                  w_up=w_up, w_down=w_down)
    past_len = page_table.shape[1] * cache_moba_k.shape[2]
    pos_p = jnp.arange(x_prompt.shape[1])
    pos_s = past_len + jnp.arange(x_sample.shape[1])
    xp, xs = x_prompt, x_sample
    new_p, new_s = [], []
    for l in range(DEPTH):
        pl = {name: arr[l] for name, arr in params.items()}
        lam_init = 0.8 - 0.6 * math.exp(-0.3 * l)
        xp, st_p = decoder_layer(xp, c_prompt, pos_p, pl, lam_init, None)
        past = (gather_pages(cache_moba_k[l], page_table), gather_pages(cache_moba_v[l], page_table),
                gather_pages(cache_diff_k[l], page_table), gather_pages(cache_diff_v[l], page_table),
                state_ssm[l], state_conv[l], state_wkv[l], state_shift[l])
        xs, st_s = decoder_layer(xs, c_sample, pos_s, pl, lam_init, past)
        new_p.append(st_p)
        new_s.append(st_s)
    mk_p, mv_p, dk_p, dv_p, ssm_p, conv_p, wkv_p, shift_p = [jnp.stack([s[i] for s in new_p]) for i in range(8)]
    mk_s, mv_s, dk_s, dv_s, ssm_s, conv_s, wkv_s, shift_s = [jnp.stack([s[i] for s in new_s]) for i in range(8)]
    return (xp, xs, mk_p, mv_p, dk_p, dv_p, ssm_p, conv_p, wkv_p, shift_p,
            mk_s, mv_s, dk_s, dv_s, ssm_s, conv_s, wkv_s, shift_s)
```

```python
import functools
import math

import jax
import jax.numpy as jnp
from jax import lax
from jax.experimental import pallas as pl
from jax.experimental.pallas import tpu as pltpu

F32 = jnp.float32
BF16 = jnp.bfloat16
HI = lax.Precision.HIGHEST

V7X_VMEM_BYTES = 64 * 1024 * 1024
LANES = 128
SUBLANES = 8

D_MODEL = 1024
GROUP_W = 256
HEAD_DIM = 64
N_HEADS = 4
SSM_STATE = 128
SSM_CONV = 4
SSM_CHUNK = 128
SSM_XBC = 768
MOBA_BLOCK = 256
MOBA_TOPK = 3
DIFF_DH = 32
RWKV_PROJ = 1024
RWKV_CHUNK = 64
RWKV_LN_EPS = 64e-5
ROPE_THETA = 500000.0
D_FF = 2816
N_MOD = 6
RMS_EPS = 1e-6
NEG_BIG = -1e30

SEG_WIDTHS = (("xbc", 768), ("z", 256), ("mq", 256), ("mk", 256), ("mv", 256), ("dq", 256), ("dk", 256),
              ("dv", 256), ("pd", 1024), ("dt", 128))
IN_PROJ_PAD = sum(w for _, w in SEG_WIDTHS)


def _cparams(sem, vmem_mib):
    return pltpu.CompilerParams(dimension_semantics=sem, vmem_limit_bytes=vmem_mib * 1024 * 1024)


def _sigmoid(x):
    return 1.0 / (1.0 + jnp.exp(-x))


def _silu(x):
    return x * _sigmoid(x)


def _softplus(x):
    return jnp.maximum(x, 0.0) + jnp.log(1.0 + jnp.exp(-jnp.abs(x)))


def _dot(a, b, precision=None):
    return jnp.dot(a, b, preferred_element_type=F32, precision=precision)


def _dot_nt(a, b, precision=None):
    return lax.dot_general(a, b, (((1,), (1,)), ((), ())), preferred_element_type=F32, precision=precision)


def _dot_tn(a, b, precision=None):
    return lax.dot_general(a, b, (((0,), (0,)), ((), ())), preferred_element_type=F32, precision=precision)


def _rms(x, w):
    return x * lax.rsqrt(jnp.mean(x * x, axis=-1, keepdims=True) + RMS_EPS) * w


def _mod_kernel(c_ref, w_ref, b_ref, o_ref):
    c = c_ref[...]
    o_ref[...] = _dot(_silu(c).astype(BF16), w_ref[...].astype(BF16)) + b_ref[...]


def _ada_mod(c_all, w_ada, b_ada):
    depth, _, n = w_ada.shape
    nb = c_all.shape[0]
    tn = 1536
    return pl.pallas_call(
        _mod_kernel,
        grid=(depth, n // tn),
        in_specs=[pl.BlockSpec((nb, D_MODEL), lambda l, j: (0, 0)),
                  pl.BlockSpec((None, D_MODEL, tn), lambda l, j: (l, 0, j)),
                  pl.BlockSpec((None, 1, tn), lambda l, j: (l, 0, j))],
        out_specs=pl.BlockSpec((None, nb, tn), lambda l, j: (l, 0, j)),
        out_shape=jax.ShapeDtypeStruct((depth, nb, n), F32),
        compiler_params=_cparams(("arbitrary", "arbitrary"), 40),
        name="ada_mod",
    )(c_all, w_ada, b_ada.reshape(depth, 1, n))


def _inproj_kernel(x_ref, mod_ref, nw_ref, w_ref, *out_refs):
    x = x_ref[...]
    h = _rms(x, nw_ref[...]) * (1.0 + mod_ref[1]) + mod_ref[0]
    hb = h.astype(BF16)
    off = 0
    for o_ref, (_, width) in zip(out_refs, SEG_WIDTHS):
        o_ref[...] = _dot(hb, w_ref[:, off:off + width])
        off += width


def _in_proj(x2d, mod4, norm_w, w_in_b, layer, tm):
    n_tok = x2d.shape[0]
    nb, _, r, _ = mod4.shape
    tiles_per_mod = (n_tok // tm) // nb
    out_shape = [jax.ShapeDtypeStruct((n_tok, w), F32) for _, w in SEG_WIDTHS]
    out_specs = [pl.BlockSpec((tm, w), lambda i: (i, 0)) for _, w in SEG_WIDTHS]
    outs = pl.pallas_call(
        _inproj_kernel,
        grid=(n_tok // tm,),
        in_specs=[pl.BlockSpec((tm, D_MODEL), lambda i: (i, 0)),
                  pl.BlockSpec((None, N_MOD, r, D_MODEL), lambda i: (i // tiles_per_mod, 0, 0, 0)),
                  pl.BlockSpec((None, 1, D_MODEL), lambda i: (layer, 0, 0)),
                  pl.BlockSpec((None, D_MODEL, IN_PROJ_PAD), lambda i: (layer, 0, 0))],
        out_specs=out_specs,
        out_shape=out_shape,
        compiler_params=_cparams(("arbitrary",), 56),
        name="in_proj",
    )(x2d, mod4, norm_w, w_in_b)
    return dict(zip([n for n, _ in SEG_WIDTHS], outs))


def _post_kernel(x_ref, ya_ref, yb_ref, yc_ref, yd_ref, mod_ref, nw_ref, wo_ref, wg_ref, wu_ref, wd_ref,
                 o_ref, x1_scr, h_scr, acc_scr):
    f = pl.program_id(1)

    @pl.when(f == 0)
    def _():
        mixed = _dot(ya_ref[...].astype(BF16), wo_ref[0:GROUP_W, :])
        mixed += _dot(yb_ref[...].astype(BF16), wo_ref[GROUP_W:2 * GROUP_W, :])
        mixed += _dot(yc_ref[...].astype(BF16), wo_ref[2 * GROUP_W:3 * GROUP_W, :])
        mixed += _dot(yd_ref[...].astype(BF16), wo_ref[3 * GROUP_W:4 * GROUP_W, :])
        x1 = x_ref[...] + (1.0 + mod_ref[2]) * _rms(mixed, nw_ref[0])
        x1_scr[...] = x1
        h = _rms(x1, nw_ref[1]) * (1.0 + mod_ref[4]) + mod_ref[3]
        h_scr[...] = h.astype(BF16)
        acc_scr[...] = jnp.zeros_like(acc_scr)

    hb = h_scr[...]
    a = _silu(_dot(hb, wg_ref[...])) * _dot(hb, wu_ref[...])
    acc_scr[...] += _dot(a.astype(BF16), wd_ref[...])

    @pl.when(f == pl.num_programs(1) - 1)
    def _():
        o_ref[...] = x1_scr[...] + (1.0 + mod_ref[5]) * _rms(acc_scr[...], nw_ref[2])


def _post(x2d, ys, mod4, norms3, w_out_b, w_gate_b, w_up_b, w_down_b, layer, tm, tf):
    n_tok = x2d.shape[0]
    nb, _, r, _ = mod4.shape
    tiles_per_mod = (n_tok // tm) // nb
    yspec = pl.BlockSpec((tm, GROUP_W), lambda i, f: (i, 0))
    return pl.pallas_call(
        _post_kernel,
        grid=(n_tok // tm, D_FF // tf),
        in_specs=[pl.BlockSpec((tm, D_MODEL), lambda i, f: (i, 0)), yspec, yspec, yspec, yspec,
                  pl.BlockSpec((None, N_MOD, r, D_MODEL), lambda i, f: (i // tiles_per_mod, 0, 0, 0)),
                  pl.BlockSpec((None, 3, 1, D_MODEL), lambda i, f: (layer, 0, 0, 0)),
                  pl.BlockSpec((None, D_MODEL, D_MODEL), lambda i, f: (layer, 0, 0)),
                  pl.BlockSpec((None, D_MODEL, tf), lambda i, f: (layer, 0, f)),
                  pl.BlockSpec((None, D_MODEL, tf), lambda i, f: (layer, 0, f)),
                  pl.BlockSpec((None, tf, D_MODEL), lambda i, f: (layer, f, 0))],
        out_specs=pl.BlockSpec((tm, D_MODEL), lambda i, f: (i, 0)),
        out_shape=jax.ShapeDtypeStruct((n_tok, D_MODEL), F32),
        scratch_shapes=[pltpu.VMEM((tm, D_MODEL), F32), pltpu.VMEM((tm, D_MODEL), BF16),
                        pltpu.VMEM((tm, D_MODEL), F32)],
        compiler_params=_cparams(("arbitrary", "arbitrary"), 56),
        name="post",
    )(x2d, ys[0], ys[1], ys[2], ys[3], mod4, norms3, w_out_b, w_gate_b, w_up_b, w_down_b)


def _ssd_kernel(xbc_ref, z_ref, dt_ref, cprev_ref, sprev_ref, cw_ref, cb_ref, hp_ref, nw_ref,
                y_ref, cnew_ref, snew_ref, buf, st, *, q):
    c = pl.program_id(1)

    @pl.when(c == 0)
    def _():
        buf[0:SUBLANES, :] = cprev_ref[...]
        st[...] = sprev_ref[...]

    xin = xbc_ref[...]
    buf[SUBLANES:SUBLANES + q, :] = xin
    conv = cb_ref[...] + cw_ref[3:4, :] * xin
    for i in range(SSM_CONV - 1):
        conv += cw_ref[i:i + 1, :] * buf[SUBLANES - 3 + i:SUBLANES - 3 + i + q, :]
    tail = buf[q:q + SUBLANES, :]
    buf[0:SUBLANES, :] = tail
    cnew_ref[...] = tail

    xc = _silu(conv)
    xs = xc[:, 0:GROUP_W]
    bm = xc[:, GROUP_W:GROUP_W + 2 * SSM_STATE]
    cm = xc[:, GROUP_W + 2 * SSM_STATE:]
    hp = hp_ref[...]
    dt = _softplus(dt_ref[...] + hp[0:1, :])
    da = dt * (-jnp.exp(hp[1:2, :]))
    rows = lax.broadcasted_iota(jnp.int32, (q, q), 0)
    cols = lax.broadcasted_iota(jnp.int32, (q, q), 1)
    tril = rows >= cols
    cs = _dot(tril.astype(F32), da, HI)
    onehot = (lax.broadcasted_iota(jnp.int32, (SUBLANES, LANES), 0)
              == lax.broadcasted_iota(jnp.int32, (SUBLANES, LANES), 1)).astype(F32)
    cs_t = _dot_nt(onehot, cs, HI)

    y_heads = []
    for g in range(2):
        bg = bm[:, g * SSM_STATE:(g + 1) * SSM_STATE]
        cg = cm[:, g * SSM_STATE:(g + 1) * SSM_STATE]
        cb = _dot_nt(cg, bg)
        for j in range(2):
            hd = 2 * g + j
            cs_col = cs[:, hd:hd + 1]
            cs_row = cs_t[hd:hd + 1, :]
            decay_in = jnp.where(tril, jnp.exp(jnp.where(tril, cs_col - cs_row, 0.0)), 0.0)
            xh = xs[:, hd * HEAD_DIM:(hd + 1) * HEAD_DIM]
            xdt = xh * dt[:, hd:hd + 1]
            s_prev = st[hd]
            y = _dot(cb * decay_in, xdt)
            y += _dot_nt(cg, s_prev) * jnp.exp(cs_col)
            cs_last = cs[q - 1:q, hd:hd + 1]
            st[hd] = jnp.exp(cs_last) * s_prev + _dot_tn(xdt * jnp.exp(cs_last - cs_col), bg)
            y_heads.append(y + hp[2:3, hd:hd + 1] * xh)
    y = jnp.concatenate(y_heads, axis=-1) * _silu(z_ref[...])
    y_ref[...] = _rms(y, nw_ref[...])

    @pl.when(c == pl.num_programs(1) - 1)
    def _():
        snew_ref[...] = st[...]


def _ssd_head_params(dt_bias, a_log, d_skip):
    rows = jnp.stack([dt_bias, a_log, d_skip]).astype(F32)
    return jnp.pad(rows, ((0, SUBLANES - 3), (0, LANES - N_HEADS)))


def _rwkv_vec_params(p):
    small = jnp.stack([p['w0'], p['a0'], p['k_k'], p['k_a'], p['r_k'].reshape(GROUP_W), p['ln_x_w'], p['ln_x_b']])
    return jnp.concatenate([p['mu_shift'][None, :], jnp.pad(small, ((0, 0), (0, RWKV_PROJ - GROUP_W)))], axis=0)


def _ssd(xbc, z, dt, conv_prev8, ssm_prev, conv_w, conv_b, head_params, norm_w, n_seq, seq_len):
    q = math.gcd(seq_len, SSM_CHUNK)
    nc = seq_len // q
    tok = lambda b, c: (b * nc + c, 0)
    fixed2 = lambda b, c: (0, 0)
    y, conv_new8, ssm_new = pl.pallas_call(
        functools.partial(_ssd_kernel, q=q),
        grid=(n_seq, nc),
        in_specs=[pl.BlockSpec((q, SSM_XBC), tok), pl.BlockSpec((q, GROUP_W), tok), pl.BlockSpec((q, LANES), tok),
                  pl.BlockSpec((None, SUBLANES, SSM_XBC), lambda b, c: (b, 0, 0)),
                  pl.BlockSpec((None, N_HEADS, HEAD_DIM, SSM_STATE), lambda b, c: (b, 0, 0, 0)),
                  pl.BlockSpec((SSM_CONV, SSM_XBC), fixed2), pl.BlockSpec((1, SSM_XBC), fixed2),
                  pl.BlockSpec((SUBLANES, LANES), fixed2), pl.BlockSpec((1, GROUP_W), fixed2)],
        out_specs=[pl.BlockSpec((q, GROUP_W), tok),
                   pl.BlockSpec((None, SUBLANES, SSM_XBC), lambda b, c: (b, 0, 0)),
                   pl.BlockSpec((None, N_HEADS, HEAD_DIM, SSM_STATE), lambda b, c: (b, 0, 0, 0))],
        out_shape=[jax.ShapeDtypeStruct((n_seq * seq_len, GROUP_W), F32),
                   jax.ShapeDtypeStruct((n_seq, SUBLANES, SSM_XBC), F32),
                   jax.ShapeDtypeStruct((n_seq, N_HEADS, HEAD_DIM, SSM_STATE), F32)],
        scratch_shapes=[pltpu.VMEM((q + SUBLANES, SSM_XBC), F32), pltpu.VMEM((N_HEADS, HEAD_DIM, SSM_STATE), F32)],
        compiler_params=_cparams(("arbitrary", "arbitrary"), 32),
        name="ssd",
    )(xbc, z, dt, conv_prev8, ssm_prev, conv_w, conv_b, head_params, norm_w)
    return y, conv_new8[:, SUBLANES - (SSM_CONV - 1):, :], ssm_new


def _rwkv_kernel(u_ref, shprev_ref, sprev_ref, vp_ref, w2_ref, a2_ref, g2_ref,
                 y_ref, shnew_ref, snew_ref, buf, st, *, c):
    ci = pl.program_id(1)

    @pl.when(ci == 0)
    def _():
        buf[0:SUBLANES, :] = jnp.broadcast_to(shprev_ref[...], (SUBLANES, RWKV_PROJ))
        st[...] = sprev_ref[...]

    u = u_ref[...]
    buf[SUBLANES:SUBLANES + c, :] = u
    u_prev = buf[SUBLANES - 1:SUBLANES - 1 + c, :]
    buf[0:SUBLANES, :] = buf[c:c + SUBLANES, :]
    shnew_ref[...] = u[c - 1:c, :]

    vp = vp_ref[...]
    us = u + vp[0:1, :] * (u_prev - u)
    gw = GROUP_W
    r = us[:, 0:gw]
    k = us[:, gw:2 * gw]
    v = us[:, 2 * gw:3 * gw]
    wd = us[:, 3 * gw:3 * gw + 64]
    ad = us[:, 3 * gw + 64:3 * gw + 128]
    gd = us[:, 3 * gw + 128:]
    w0, a0, k_k, k_a = vp[1:2, 0:gw], vp[2:3, 0:gw], vp[3:4, 0:gw], vp[4:5, 0:gw]
    r_k, ln_w, ln_b = vp[5:6, 0:gw], vp[6:7, 0:gw], vp[7:8, 0:gw]
    w = -_softplus(-(w0 + _dot(jnp.tanh(wd), w2_ref[...]))) - 0.5
    a = _sigmoid(a0 + _dot(ad, a2_ref[...]))
    g = _dot(_sigmoid(gd), g2_ref[...])
    kk_raw = k * k_k
    k2 = k * (1.0 + (a - 1.0) * k_a)
    lw = -jnp.exp(w)

    rows = lax.broadcasted_iota(jnp.int32, (c, c), 0)
    cols = lax.broadcasted_iota(jnp.int32, (c, c), 1)
    tril = rows >= cols
    stril = rows > cols
    cum = _dot(tril.astype(F32), lw, HI)
    e_in = jnp.exp(cum)
    e_out = jnp.exp(-cum)
    e_prev = jnp.exp(cum - lw)
    eye = (rows == cols).astype(F32)
    levels = int(math.log2(c))

    y_heads = []
    for h in range(N_HEADS):
        sl = slice(h * HEAD_DIM, (h + 1) * HEAD_DIM)
        kk = kk_raw[:, sl]
        kk = kk / jnp.maximum(jnp.sqrt(jnp.sum(kk * kk, axis=-1, keepdims=True)), 1e-12)
        rh, kh, vh, ah = r[:, sl], k2[:, sl], v[:, sl], a[:, sl]
        r_t = rh * e_in[:, sl]
        k_t = kh * e_out[:, sl]
        b_t = kk * ah * e_out[:, sl]
        kap_t = kk * e_prev[:, sl]
        aa = _dot_nt(jnp.concatenate([kap_t, r_t], axis=0), jnp.concatenate([k_t, b_t], axis=0))
        a_kk = jnp.where(stril, aa[0:c, 0:c], 0.0)
        a_kb = jnp.where(stril, aa[0:c, c:2 * c], 0.0)
        a_rk = jnp.where(tril, aa[c:2 * c, 0:c], 0.0)
        a_rb = jnp.where(tril, aa[c:2 * c, c:2 * c], 0.0)
        t_inv = eye - a_kb
        n_pow = a_kb
        for _ in range(levels - 1):
            n_pow = _dot(n_pow, n_pow)
            t_inv = t_inv + _dot(t_inv, n_pow)
        s_prev = st[h]
        wu = _dot(t_inv, jnp.concatenate([kap_t, _dot(a_kk, vh)], axis=-1))
        uu = _dot_nt(wu[:, 0:HEAD_DIM], s_prev) + wu[:, HEAD_DIM:]
        yh = _dot_nt(r_t, s_prev) + _dot(a_rk, vh) - _dot(a_rb, uu)
        upd = _dot_tn(jnp.concatenate([vh, uu], axis=0), jnp.concatenate([k_t, -b_t], axis=0))
        st[h] = (s_prev + upd) * e_in[c - 1:c, sl]
        mu = jnp.mean(yh, axis=-1, keepdims=True)
        var = jnp.mean((yh - mu) ** 2, axis=-1, keepdims=True)
        yn = (yh - mu) * lax.rsqrt(var + RWKV_LN_EPS)
        bonus = jnp.sum(rh * kh * r_k[:, sl], axis=-1, keepdims=True) * vh
        y_heads.append((yn, bonus))
    yn = jnp.concatenate([p[0] for p in y_heads], axis=-1)
    bonus = jnp.concatenate([p[1] for p in y_heads], axis=-1)
    y_ref[...] = (yn * ln_w + ln_b + bonus) * g

    @pl.when(ci == pl.num_programs(1) - 1)
    def _():
        snew_ref[...] = st[...]


def _rwkv(pd, shift_prev, wkv_prev, vec_params, w2, a2, g2, n_seq, seq_len):
    c = math.gcd(seq_len, RWKV_CHUNK)
    nc = seq_len // c
    tok = lambda b, i: (b * nc + i, 0)
    fixed2 = lambda b, i: (0, 0)
    y, shift_new, wkv_new = pl.pallas_call(
        functools.partial(_rwkv_kernel, c=c),
        grid=(n_seq, nc),
        in_specs=[pl.BlockSpec((c, RWKV_PROJ), tok),
                  pl.BlockSpec((None, 1, RWKV_PROJ), lambda b, i: (b, 0, 0)),
                  pl.BlockSpec((None, N_HEADS, HEAD_DIM, HEAD_DIM), lambda b, i: (b, 0, 0, 0)),
                  pl.BlockSpec((SUBLANES, RWKV_PROJ), fixed2),
                  pl.BlockSpec((64, GROUP_W), fixed2), pl.BlockSpec((64, GROUP_W), fixed2),
                  pl.BlockSpec((128, GROUP_W), fixed2)],
        out_specs=[pl.BlockSpec((c, GROUP_W), tok),
                   pl.BlockSpec((None, 1, RWKV_PROJ), lambda b, i: (b, 0, 0)),
                   pl.BlockSpec((None, N_HEADS, HEAD_DIM, HEAD_DIM), lambda b, i: (b, 0, 0, 0))],
        out_shape=[jax.ShapeDtypeStruct((n_seq * seq_len, GROUP_W), F32),
                   jax.ShapeDtypeStruct((n_seq, 1, RWKV_PROJ), F32),
                   jax.ShapeDtypeStruct((n_seq, N_HEADS, HEAD_DIM, HEAD_DIM), F32)],
        scratch_shapes=[pltpu.VMEM((c + SUBLANES, RWKV_PROJ), F32), pltpu.VMEM((N_HEADS, HEAD_DIM, HEAD_DIM), F32)],
        compiler_params=_cparams(("arbitrary", "arbitrary"), 32),
        name="rwkv",
    )(pd, shift_prev, wkv_prev, vec_params, w2, a2, g2)
    return y, shift_new[:, 0, :], wkv_new


def _rope_tables(pos, unit):
    rot = unit // 4
    half = rot // 2
    inv_freq = ROPE_THETA ** (-jnp.arange(half, dtype=jnp.float32) / half)
    ang = pos.astype(jnp.float32)[:, None] * inv_freq[None, :]
    cos, sin = jnp.cos(ang), jnp.sin(ang)
    t = pos.shape[0]
    zero_h = jnp.zeros((t, half), F32)
    rest0 = jnp.zeros((t, unit - rot), F32)
    c = jnp.concatenate([cos, cos, jnp.ones((t, unit - rot), F32)], axis=-1)
    sa = jnp.concatenate([-sin, zero_h, rest0], axis=-1)
    sb = jnp.concatenate([zero_h, sin, rest0], axis=-1)
    reps = GROUP_W // unit
    return tuple(jnp.tile(a, (1, reps)) for a in (c, sa, sb))


def _rope(x, c, sa, sb, half):
    n = x.shape[-1]
    return x * c + pltpu.roll(x, n - half, 1) * sa + pltpu.roll(x, half, 1) * sb


def _moba_prompt_kernel(q_ref, k_ref, v_ref, c_ref, sa_ref, sb_ref, y_ref, kout_ref, km_scr, *, nb):
    i = pl.program_id(1)
    blk = MOBA_BLOCK
    half = HEAD_DIM // 8

    @pl.when(i == 0)
    def _():
        kr = _rope(k_ref[...], c_ref[...], sa_ref[...], sb_ref[...], half)
        kout_ref[...] = kr
        km_scr[...] = jnp.zeros_like(km_scr)
        for n in range(nb):
            km_scr[n:n + 1, :] = jnp.mean(kr[n * blk:(n + 1) * blk, :], axis=0, keepdims=True)

    start = pl.multiple_of(i * blk, blk)
    q = _rope(q_ref[...], c_ref[pl.ds(start, blk), :], sa_ref[pl.ds(start, blk), :], sb_ref[pl.ds(start, blk), :], half)
    scale = HEAD_DIM ** -0.5
    lane = lax.broadcasted_iota(jnp.int32, (blk, LANES), 1)
    rows = lax.broadcasted_iota(jnp.int32, (blk, blk), 0)
    cols = lax.broadcasted_iota(jnp.int32, (blk, blk), 1)
    outs = []
    for h in range(N_HEADS):
        sl = slice(h * HEAD_DIM, (h + 1) * HEAD_DIM)
        qh = q[:, sl]
        gate = jnp.where(lane < i, _dot_nt(qh, km_scr[:, sl], HI), -jnp.inf)
        cnt = jnp.zeros((blk, LANES), F32)
        for m in range(nb):
            gm = gate[:, m:m + 1]
            cnt += jnp.where((gm > gate) | ((gm == gate) & (m < lane)), 1.0, 0.0)
        sel = jnp.where((lane < i) & (cnt < MOBA_TOPK), 1.0, 0.0)

        s = _dot_nt(qh, kout_ref[pl.ds(start, blk), :][:, sl]) * scale
        s = jnp.where(cols <= rows, s, NEG_BIG)
        m0 = jnp.max(s, axis=-1, keepdims=True)
        p = jnp.exp(s - m0)
        l0 = jnp.sum(p, axis=-1, keepdims=True)
        acc0 = _dot(p, v_ref[pl.ds(start, blk), :][:, sl])

        def body(n, carry, qh=qh, sel=sel, sl=sl):
            m_run, l_run, acc = carry
            off = pl.multiple_of(n * blk, blk)
            s = _dot_nt(qh, kout_ref[pl.ds(off, blk), :][:, sl]) * scale
            chosen = jnp.sum(jnp.where(lane == n, sel, 0.0), axis=-1, keepdims=True) > 0.5
            s = jnp.where(chosen, s, NEG_BIG)
            m_new = jnp.maximum(m_run, jnp.max(s, axis=-1, keepdims=True))
            alpha = jnp.exp(m_run - m_new)
            p = jnp.exp(s - m_new)
            l_new = alpha * l_run + jnp.sum(p, axis=-1, keepdims=True)
            acc = alpha * acc + _dot(p, v_ref[pl.ds(off, blk), :][:, sl])
            return m_new, l_new, acc

        _, l_fin, acc_fin = lax.fori_loop(0, i, body, (m0, l0, acc0))
        outs.append(acc_fin / l_fin)
    y_ref[...] = jnp.concatenate(outs, axis=-1)


def _moba_prompt(q, k, v, tabs, n_seq, seq_len):
    nb = seq_len // MOBA_BLOCK
    seq = lambda b, i: (b, 0)
    fixed = lambda b, i: (0, 0)
    full = pl.BlockSpec((seq_len, GROUP_W), seq)
    tab = pl.BlockSpec((seq_len, GROUP_W), fixed)
    y, k_rot = pl.pallas_call(
        functools.partial(_moba_prompt_kernel, nb=nb),
        grid=(n_seq, nb),
        in_specs=[pl.BlockSpec((MOBA_BLOCK, GROUP_W), lambda b, i: (b * nb + i, 0)), full, full, tab, tab, tab],
        out_specs=[pl.BlockSpec((MOBA_BLOCK, GROUP_W), lambda b, i: (b * nb + i, 0)), full],
        out_shape=[jax.ShapeDtypeStruct((n_seq * seq_len, GROUP_W), F32)] * 2,
        scratch_shapes=[pltpu.VMEM((LANES, GROUP_W), F32)],
        compiler_params=_cparams(("arbitrary", "arbitrary"), 48),
        name="moba_prompt",
    )(q, k, v, *tabs)
    return y, k_rot


def _diff_params(p):
    rows = [jnp.pad(p[n], (0, LANES - DIFF_DH)) for n in ('lambda_q1', 'lambda_k1', 'lambda_q2', 'lambda_k2')]
    rows.append(jnp.pad(p['subln_w'], (0, LANES - 2 * DIFF_DH)))
    return jnp.pad(jnp.stack(rows), ((0, SUBLANES - 5), (0, 0)))


def _diff_lambda(dp, lam_init):
    return (jnp.exp(jnp.sum(dp[0:1, :] * dp[1:2, :], axis=-1, keepdims=True))
            - jnp.exp(jnp.sum(dp[2:3, :] * dp[3:4, :], axis=-1, keepdims=True)) + lam_init)


def _diff_prompt_kernel(q_ref, k_ref, v_ref, c_ref, sa_ref, sb_ref, dp_ref, y_ref, kout_ref, *, lam_init, tq):
    i = pl.program_id(1)
    half = DIFF_DH // 8

    @pl.when(i == 0)
    def _():
        kout_ref[...] = _rope(k_ref[...], c_ref[...], sa_ref[...], sb_ref[...], half)

    start = pl.multiple_of(i * tq, tq)
    q = _rope(q_ref[...], c_ref[pl.ds(start, tq), :], sa_ref[pl.ds(start, tq), :], sb_ref[pl.ds(start, tq), :], half)
    dp = dp_ref[...]
    lam = _diff_lambda(dp, lam_init)
    subln = dp[4:5, 0:2 * DIFF_DH]
    scale = DIFF_DH ** -0.5
    rows = lax.broadcasted_iota(jnp.int32, (tq, tq), 0)
    cols = lax.broadcasted_iota(jnp.int32, (tq, tq), 1)
    outs = []
    for h in range(N_HEADS):
        sl = slice(h * HEAD_DIM, (h + 1) * HEAD_DIM)
        qs = (q[:, h * HEAD_DIM:h * HEAD_DIM + DIFF_DH], q[:, h * HEAD_DIM + DIFF_DH:(h + 1) * HEAD_DIM])

        def step(off, causal, carry, qs=qs, sl=sl):
            kb = kout_ref[pl.ds(off, tq), :][:, sl]
            vb = v_ref[pl.ds(off, tq), :][:, sl]
            new = []
            for j in range(2):
                s = _dot_nt(qs[j], kb[:, j * DIFF_DH:(j + 1) * DIFF_DH]) * scale
                if causal:
                    s = jnp.where(cols <= rows, s, NEG_BIG)
                    m_new = jnp.max(s, axis=-1, keepdims=True)
                    p = jnp.exp(s - m_new)
                    new += [m_new, jnp.sum(p, axis=-1, keepdims=True), _dot(p, vb)]
                else:
                    m_run, l_run, acc = carry[3 * j:3 * j + 3]
                    m_new = jnp.maximum(m_run, jnp.max(s, axis=-1, keepdims=True))
                    alpha = jnp.exp(m_run - m_new)
                    p = jnp.exp(s - m_new)
                    new += [m_new, alpha * l_run + jnp.sum(p, axis=-1, keepdims=True), alpha * acc + _dot(p, vb)]
            return tuple(new)

        init = step(start, True, None)
        fin = lax.fori_loop(0, i, lambda n, carry: step(pl.multiple_of(n * tq, tq), False, carry), init)
        o = fin[2] / fin[1] - lam * (fin[5] / fin[4])
        outs.append(_rms(o, subln) * (1.0 - lam_init))
    y_ref[...] = jnp.concatenate(outs, axis=-1)


def _diff_prompt(q, k, v, tabs, dparams, lam_init, n_seq, seq_len):
    tq = 256
    nt = seq_len // tq
    seq = lambda b, i: (b, 0)
    fixed = lambda b, i: (0, 0)
    full = pl.BlockSpec((seq_len, GROUP_W), seq)
    tab = pl.BlockSpec((seq_len, GROUP_W), fixed)
    y, k_rows = pl.pallas_call(
        functools.partial(_diff_prompt_kernel, lam_init=float(lam_init), tq=tq),
        grid=(n_seq, nt),
        in_specs=[pl.BlockSpec((tq, GROUP_W), lambda b, i: (b * nt + i, 0)), full, full, tab, tab, tab,
                  pl.BlockSpec((SUBLANES, LANES), fixed)],
        out_specs=[pl.BlockSpec((tq, GROUP_W), lambda b, i: (b * nt + i, 0)), full],
        out_shape=[jax.ShapeDtypeStruct((n_seq * seq_len, GROUP_W), F32)] * 2,
        compiler_params=_cparams(("arbitrary", "arbitrary"), 48),
        name="diff_prompt",
    )(q, k, v, *tabs, dparams)
    return y, k_rows


SAMPLE_T = 8
NQ_MOBA = N_HEADS * SAMPLE_T
NQ_DIFF = 2 * N_HEADS * SAMPLE_T


def _block_diag_queries(q, n_maps):
    nq = n_maps * N_HEADS * SAMPLE_T
    tiled = jnp.concatenate([q] * (nq // SAMPLE_T) + [jnp.zeros((LANES - nq, GROUP_W), F32)], axis=0)
    row = lax.broadcasted_iota(jnp.int32, (LANES, GROUP_W), 0)
    lane = lax.broadcasted_iota(jnp.int32, (LANES, GROUP_W), 1)
    width = HEAD_DIM // n_maps
    keep = (row < nq) & ((row // SAMPLE_T) % N_HEADS == lane // HEAD_DIM) & (row // (N_HEADS * SAMPLE_T) == (lane % HEAD_DIM) // width)
    return jnp.where(keep, tiled, 0.0)


def _block_partials(kb, vb, qblk, scale, nq, mask=None):
    s = _dot_nt(kb, qblk) * scale
    if mask is not None:
        s = jnp.where(mask, s, NEG_BIG)
    m = jnp.max(s, axis=0, keepdims=True)
    p = jnp.exp(s - m)
    l = jnp.sum(p, axis=0, keepdims=True)
    return m, l, _dot_tn(p[:, 0:nq], vb)


def _merge_blocks(m_scr, l_scr, o_scr, w_scr, sel, own, nblk, nq):
    m_own, l_own, o_own = own
    m_all = m_scr[0:nblk, :]
    m_tot = jnp.maximum(jnp.max(jnp.where(sel, m_all, NEG_BIG), axis=0, keepdims=True), m_own)
    w = jnp.where(sel, jnp.exp(m_all - m_tot), 0.0)
    e_own = jnp.exp(m_own - m_tot)
    den = jnp.sum(w * l_scr[0:nblk, :], axis=0, keepdims=True) + e_own * l_own
    w_scr[...] = jnp.zeros_like(w_scr)
    w_scr[0:nblk, :] = w
    w_scr[nblk:nblk + 1, :] = e_own
    w_scr[nblk + 1:nblk + 2, :] = den
    wt = w_scr[...].T
    num = wt[0:nq, nblk:nblk + 1] * o_own
    for n in range(nblk):
        num += wt[0:nq, n:n + 1] * o_scr[n]
    return num / wt[0:nq, nblk + 1:nblk + 2]


def _sample_attn_kernel(pt_ref, mq_ref, mk_ref, mv_ref, dq_ref, dk_ref, dv_ref,
                        mka_ref, mkb_ref, mva_ref, mvb_ref, dka_ref, dkb_ref, dva_ref, dvb_ref,
                        mc_ref, msa_ref, msb_ref, dc_ref, dsa_ref, dsb_ref, dp_ref,
                        ym_ref, yd_ref, mkout_ref, dkout_ref,
                        qm_scr, qd_scr, g_scr, mm_scr, lm_scr, om_scr, md_scr, ld_scr, od_scr, w_scr,
                        *, nblk, lam_init):
    n = pl.program_id(1)
    m_half = HEAD_DIM // 8
    d_half = DIFF_DH // 8
    m_scale = HEAD_DIM ** -0.5
    d_scale = DIFF_DH ** -0.5

    @pl.when(n == 0)
    def _():
        qm = _rope(mq_ref[...], mc_ref[...], msa_ref[...], msb_ref[...], m_half)
        qm_scr[...] = _block_diag_queries(qm, 1)
        qd = _rope(dq_ref[...], dc_ref[...], dsa_ref[...], dsb_ref[...], d_half)
        qd_scr[...] = _block_diag_queries(qd, 2)

    kb = jnp.concatenate([mka_ref[...], mkb_ref[...]], axis=0)
    vb = jnp.concatenate([mva_ref[...], mvb_ref[...]], axis=0)
    qm_blk = qm_scr[...]
    m, l, o = _block_partials(kb, vb, qm_blk, m_scale, NQ_MOBA)
    mm_scr[pl.ds(n, 1), :] = m
    lm_scr[pl.ds(n, 1), :] = l
    om_scr[n] = o
    k_mean = jnp.broadcast_to(jnp.mean(kb, axis=0, keepdims=True), (SUBLANES, GROUP_W))
    g_scr[pl.ds(n, 1), :] = _dot_nt(k_mean, qm_blk, HI)[0:1, :]

    kb = jnp.concatenate([dka_ref[...], dkb_ref[...]], axis=0)
    vb = jnp.concatenate([dva_ref[...], dvb_ref[...]], axis=0)
    m, l, o = _block_partials(kb, vb, qd_scr[...], d_scale, NQ_DIFF)
    md_scr[pl.ds(n, 1), :] = m
    ld_scr[pl.ds(n, 1), :] = l
    od_scr[n] = o

    @pl.when(n == nblk - 1)
    def _():
        key = lax.broadcasted_iota(jnp.int32, (SAMPLE_T, LANES), 0)
        qry = lax.broadcasted_iota(jnp.int32, (SAMPLE_T, LANES), 1) % SAMPLE_T
        causal = key <= qry
        blk_row = lax.broadcasted_iota(jnp.int32, (nblk, LANES), 0)

        k_new = _rope(mk_ref[...], mc_ref[...], msa_ref[...], msb_ref[...], m_half)
        mkout_ref[...] = k_new
        own = _block_partials(k_new, mv_ref[...], qm_scr[...], m_scale, NQ_MOBA, causal)
        gate = g_scr[0:nblk, :]
        cnt = jnp.zeros((nblk, LANES), F32)
        for j in range(nblk):
            gj = gate[j:j + 1, :]
            cnt += jnp.where((gj > gate) | ((gj == gate) & (j < blk_row)), 1.0, 0.0)
        rows_m = _merge_blocks(mm_scr, lm_scr, om_scr, w_scr, cnt < MOBA_TOPK, own, nblk, NQ_MOBA)
        ym_ref[...] = jnp.concatenate(
            [rows_m[h * SAMPLE_T:(h + 1) * SAMPLE_T, h * HEAD_DIM:(h + 1) * HEAD_DIM] for h in range(N_HEADS)], axis=-1)

        k_new = _rope(dk_ref[...], dc_ref[...], dsa_ref[...], dsb_ref[...], d_half)
        dkout_ref[...] = k_new
        own = _block_partials(k_new, dv_ref[...], qd_scr[...], d_scale, NQ_DIFF, causal)
        rows_d = _merge_blocks(md_scr, ld_scr, od_scr, w_scr, blk_row >= 0, own, nblk, NQ_DIFF)
        dp = dp_ref[...]
        lam = _diff_lambda(dp, lam_init)
        subln = dp[4:5, 0:2 * DIFF_DH]
        outs = []
        for h in range(N_HEADS):
            o1 = rows_d[h * SAMPLE_T:(h + 1) * SAMPLE_T, h * HEAD_DIM:(h + 1) * HEAD_DIM]
            o2 = rows_d[NQ_MOBA + h * SAMPLE_T:NQ_MOBA + (h + 1) * SAMPLE_T, h * HEAD_DIM:(h + 1) * HEAD_DIM]
            outs.append(_rms(o1 - lam * o2, subln) * (1.0 - lam_init))
        yd_ref[...] = jnp.concatenate(outs, axis=-1)


def _sample_attn(mq, mk, mv, dq, dk, dv, caches, page_table, tabs_m, tabs_d, dparams, lam_init, layer, n_seq):
    n_pages = page_table.shape[1]
    page = caches[0].shape[2]
    per_blk = MOBA_BLOCK // page
    nblk = n_pages // per_blk
    depth, n_pool = caches[0].shape[:2]
    pools = [c.reshape(depth, n_pool, page, GROUP_W) for c in caches]
    new = pl.BlockSpec((SAMPLE_T, GROUP_W), lambda b, n, pt: (b, 0))
    tab = pl.BlockSpec((SAMPLE_T, GROUP_W), lambda b, n, pt: (0, 0))

    def page_spec(j):
        return pl.BlockSpec((None, None, page, GROUP_W), lambda b, n, pt: (layer, pt[b, per_blk * n + j], 0, 0))

    cache_specs, cache_args = [], []
    for pool in pools:
        for j in range(per_blk):
            cache_specs.append(page_spec(j))
            cache_args.append(pool)
    out_sd = jax.ShapeDtypeStruct((n_seq * SAMPLE_T, GROUP_W), F32)
    stat = pltpu.VMEM((LANES, LANES), F32)
    grid_spec = pltpu.PrefetchScalarGridSpec(
        num_scalar_prefetch=1,
        grid=(n_seq, nblk),
        in_specs=[new] * 6 + cache_specs + [tab] * 6 + [pl.BlockSpec((SUBLANES, LANES), lambda b, n, pt: (0, 0))],
        out_specs=[new] * 4,
        scratch_shapes=[pltpu.VMEM((LANES, GROUP_W), F32), pltpu.VMEM((LANES, GROUP_W), F32),
                        stat, stat, stat, pltpu.VMEM((nblk, NQ_MOBA, GROUP_W), F32),
                        stat, stat, pltpu.VMEM((nblk, NQ_DIFF, GROUP_W), F32), stat],
    )
    return pl.pallas_call(
        functools.partial(_sample_attn_kernel, nblk=nblk, lam_init=float(lam_init)),
        grid_spec=grid_spec,
        out_shape=[out_sd] * 4,
        compiler_params=_cparams(("arbitrary", "arbitrary"), 32),
        name="sample_attn",
    )(page_table, mq, mk, mv, dq, dk, dv, *cache_args, *tabs_m, *tabs_d, dparams)


def _reorder_w_in(w_in):
    o_a = GROUP_W + SSM_XBC + N_HEADS
    xbc = w_in[..., GROUP_W:GROUP_W + SSM_XBC]
    z = w_in[..., 0:GROUP_W]
    dt = jnp.pad(w_in[..., GROUP_W + SSM_XBC:o_a], ((0, 0), (0, 0), (0, LANES - N_HEADS)))
    rest = w_in[..., o_a:]
    return jnp.concatenate([xbc, z, rest, dt], axis=-1)


def kernel(x_prompt, x_sample, c_prompt, c_sample, cache_moba_k, cache_moba_v, cache_diff_k, cache_diff_v, page_table, state_ssm, state_conv, state_wkv, state_shift, w_ada, b_ada, norm_mix_pre, norm_mix_post, norm_ffn_pre, norm_ffn_post, w_in, w_out, conv_w, conv_b, dt_bias, a_log, d_skip, ssm_norm_w, lambda_q1, lambda_k1, lambda_q2, lambda_k2, subln_w, mu_shift, w0, w2, a0, a2, g2, k_k, k_a, r_k, ln_x_w, ln_x_b, w_gate, w_up, w_down):
    nbp, tp, d = x_prompt.shape
    nbs, ts, _ = x_sample.shape
    depth = w_in.shape[0]
    assert ts == SAMPLE_T and d == D_MODEL and tp % MOBA_BLOCK == 0
    past_len = page_table.shape[1] * cache_moba_k.shape[2]
    assert past_len % MOBA_BLOCK == 0

    w_in_b = _reorder_w_in(w_in).astype(BF16)
    w_out_b, w_gate_b, w_up_b, w_down_b = (w.astype(BF16) for w in (w_out, w_gate, w_up, w_down))
    mods = _ada_mod(jnp.concatenate([c_prompt, c_sample], axis=0), w_ada, b_ada)
    norm_pre = norm_mix_pre.reshape(depth, 1, d)
    norms3 = jnp.stack([norm_mix_post, norm_ffn_pre, norm_ffn_post], axis=1).reshape(depth, 3, 1, d)
    pos_p = jnp.arange(tp)
    pos_s = past_len + jnp.arange(ts)
    tabs_mp, tabs_dp = _rope_tables(pos_p, HEAD_DIM), _rope_tables(pos_p, DIFF_DH)
    tabs_ms, tabs_ds = _rope_tables(pos_s, HEAD_DIM), _rope_tables(pos_s, DIFF_DH)
    caches = [cache_moba_k, cache_moba_v, cache_diff_k, cache_diff_v]

    xp = x_prompt.reshape(nbp * tp, d)
    xs = x_sample.reshape(nbs * ts, d)
    zeros_conv = jnp.zeros((nbp, SUBLANES, SSM_XBC), F32)
    zeros_ssm = jnp.zeros((nbp, N_HEADS, HEAD_DIM, SSM_STATE), F32)
    zeros_shift = jnp.zeros((nbp, 1, RWKV_PROJ), F32)
    zeros_wkv = jnp.zeros((nbp, N_HEADS, HEAD_DIM, HEAD_DIM), F32)
    new_p, new_s = [], []
    for l in range(depth):
        lam_init = 0.8 - 0.6 * math.exp(-0.3 * l)
        head_params = _ssd_head_params(dt_bias[l], a_log[l], d_skip[l])
        vec_params = _rwkv_vec_params(dict(mu_shift=mu_shift[l], w0=w0[l], a0=a0[l], k_k=k_k[l], k_a=k_a[l],
                                           r_k=r_k[l], ln_x_w=ln_x_w[l], ln_x_b=ln_x_b[l]))
        dparams = _diff_params(dict(lambda_q1=lambda_q1[l], lambda_k1=lambda_k1[l], lambda_q2=lambda_q2[l],
                                    lambda_k2=lambda_k2[l], subln_w=subln_w[l]))
        ssd_w = (conv_w[l], conv_b[l][None], head_params, ssm_norm_w[l][None])
        rwkv_w = (vec_params, w2[l], a2[l], g2[l])
        dense_w = (norms3, w_out_b, w_gate_b, w_up_b, w_down_b)

        mod_p = mods[l, :nbp].reshape(nbp, N_MOD, 1, d)
        pr = _in_proj(xp, mod_p, norm_pre, w_in_b, l, 512)
        ya, conv_p, ssm_p = _ssd(pr['xbc'], pr['z'], pr['dt'], zeros_conv, zeros_ssm, *ssd_w, nbp, tp)
        yb, mk_p = _moba_prompt(pr['mq'], pr['mk'], pr['mv'], tabs_mp, nbp, tp)
        yc, dk_p = _diff_prompt(pr['dq'], pr['dk'], pr['dv'], tabs_dp, dparams, lam_init, nbp, tp)
        yd, shift_p, wkv_p = _rwkv(pr['pd'], zeros_shift, zeros_wkv, *rwkv_w, nbp, tp)
        xp = _post(xp, (ya, yb, yc, yd), mod_p, *dense_w, l, 512, 256)
        new_p.append((mk_p, pr['mv'], dk_p, pr['dv'], ssm_p, conv_p, wkv_p, shift_p))

        mod_s = jnp.repeat(mods[l, nbp:].reshape(nbs, N_MOD, d), ts, axis=0).transpose(1, 0, 2)[None]
        sr = _in_proj(xs, mod_s, norm_pre, w_in_b, l, nbs * ts)
        conv_prev8 = jnp.pad(state_conv[l], ((0, 0), (SUBLANES - (SSM_CONV - 1), 0), (0, 0)))
        ya, conv_s, ssm_s = _ssd(sr['xbc'], sr['z'], sr['dt'], conv_prev8, state_ssm[l], *ssd_w, nbs, ts)
        yb, yc, mk_s, dk_s = _sample_attn(sr['mq'], sr['mk'], sr['mv'], sr['dq'], sr['dk'], sr['dv'], caches,
                                          page_table, tabs_ms, tabs_ds, dparams, lam_init, l, nbs)
        yd, shift_s, wkv_s = _rwkv(sr['pd'], state_shift[l][:, None, :], state_wkv[l], *rwkv_w, nbs, ts)
        xs = _post(xs, (ya, yb, yc, yd), mod_s, *dense_w, l, nbs * ts, 256)
        new_s.append((mk_s, sr['mv'], dk_s, sr['dv'], ssm_s, conv_s, wkv_s, shift_s))

    def collect(states, nb, t):
        kv_shape = (depth, nb, t, N_HEADS, HEAD_DIM)
        stacked = [jnp.stack([s[i] for s in states]) for i in range(8)]
        return tuple(a.reshape(kv_shape) for a in stacked[:4]) + tuple(stacked[4:])

    return ((xp.reshape(nbp, tp, d), xs.reshape(nbs, ts, d)) + collect(new_p, nbp, tp) + collect(new_s, nbs, ts))
```

```python
import functools
import math

import jax
import jax.numpy as jnp
from jax import lax
from jax.experimental import pallas as pl
from jax.experimental.pallas import tpu as pltpu

F32 = jnp.float32
BF16 = jnp.bfloat16
HI = lax.Precision.HIGHEST

V7X_VMEM_BYTES = 64 * 1024 * 1024
LANES = 128
SUBLANES = 8

D_MODEL = 1024
GROUP_W = 256
HEAD_DIM = 64
N_HEADS = 4
SSM_STATE = 128
SSM_CONV = 4
SSM_CHUNK = 128
SSM_XBC = 768
MOBA_BLOCK = 256
MOBA_TOPK = 3
DIFF_DH = 32
RWKV_PROJ = 1024
RWKV_CHUNK = 64
RWKV_LN_EPS = 64e-5
ROPE_THETA = 500000.0
D_FF = 2816
N_MOD = 6
RMS_EPS = 1e-6
NEG_BIG = -1e30

SEG_WIDTHS = (("xbc", 768), ("z", 256), ("mq", 256), ("mk", 256), ("mv", 256), ("dq", 256), ("dk", 256),
              ("dv", 256), ("pd", 1024), ("dt", 128))
IN_PROJ_PAD = sum(w for _, w in SEG_WIDTHS)
ROW_TILE = 512
FF_TILE = 1408


def _cparams(sem, vmem_mib):
    return pltpu.CompilerParams(dimension_semantics=sem, vmem_limit_bytes=vmem_mib * 1024 * 1024)


def _sigmoid(x):
    return 1.0 / (1.0 + jnp.exp(-x))


def _silu(x):
    return x * _sigmoid(x)


def _softplus(x):
    return jnp.maximum(x, 0.0) + jnp.log(1.0 + jnp.exp(-jnp.abs(x)))


def _mxu_operands(a, b, precision):
    if precision is None:
        return a.astype(BF16), b.astype(BF16)
    return a, b


def _dot(a, b, precision=None):
    a, b = _mxu_operands(a, b, precision)
    return jnp.dot(a, b, preferred_element_type=F32, precision=precision)


def _dot_nt(a, b, precision=None):
    a, b = _mxu_operands(a, b, precision)
    return lax.dot_general(a, b, (((1,), (1,)), ((), ())), preferred_element_type=F32, precision=precision)


def _dot_tn(a, b, precision=None):
    a, b = _mxu_operands(a, b, precision)
    return lax.dot_general(a, b, (((0,), (0,)), ((), ())), preferred_element_type=F32, precision=precision)


def _rms(x, w):
    return x * lax.rsqrt(jnp.mean(x * x, axis=-1, keepdims=True) + RMS_EPS) * w


def _mod_kernel(c_ref, w_ref, b_ref, o_ref):
    c = c_ref[...]
    o_ref[...] = _dot(_silu(c).astype(BF16), w_ref[...].astype(BF16)) + b_ref[...]


def _ada_mod(c_all, w_ada, b_ada):
    depth, _, n = w_ada.shape
    nb = c_all.shape[0]
    tn = 1536
    return pl.pallas_call(
        _mod_kernel,
        grid=(depth, n // tn),
        in_specs=[pl.BlockSpec((nb, D_MODEL), lambda l, j: (0, 0)),
                  pl.BlockSpec((None, D_MODEL, tn), lambda l, j: (l, 0, j)),
                  pl.BlockSpec((None, 1, tn), lambda l, j: (l, 0, j))],
        out_specs=pl.BlockSpec((None, nb, tn), lambda l, j: (l, 0, j)),
        out_shape=jax.ShapeDtypeStruct((depth, nb, n), F32),
        compiler_params=_cparams(("arbitrary", "arbitrary"), 40),
        name="ada_mod",
    )(c_all, w_ada, b_ada.reshape(depth, 1, n))


def _inproj_kernel(x_ref, mod_ref, nw_ref, w_ref, *out_refs):
    x = x_ref[...]
    h = _rms(x, nw_ref[...]) * (1.0 + mod_ref[1]) + mod_ref[0]
    hb = h.astype(BF16)
    off = 0
    for o_ref, (_, width) in zip(out_refs, SEG_WIDTHS):
        o_ref[...] = _dot(hb, w_ref[:, off:off + width])
        off += width


def _in_proj(x2d, mod4, norm_w, w_in_b, layer, tm):
    n_tok = x2d.shape[0]
    nb, _, r, _ = mod4.shape
    tiles_per_mod = (n_tok // tm) // nb
    out_shape = [jax.ShapeDtypeStruct((n_tok, w), F32) for _, w in SEG_WIDTHS]
    out_specs = [pl.BlockSpec((tm, w), lambda i: (i, 0)) for _, w in SEG_WIDTHS]
    outs = pl.pallas_call(
        _inproj_kernel,
        grid=(n_tok // tm,),
        in_specs=[pl.BlockSpec((tm, D_MODEL), lambda i: (i, 0)),
                  pl.BlockSpec((None, N_MOD, r, D_MODEL), lambda i: (i // tiles_per_mod, 0, 0, 0)),
                  pl.BlockSpec((None, 1, D_MODEL), lambda i: (layer, 0, 0)),
                  pl.BlockSpec((None, D_MODEL, IN_PROJ_PAD), lambda i: (layer, 0, 0))],
        out_specs=out_specs,
        out_shape=out_shape,
        compiler_params=_cparams(("arbitrary",), 56),
        name="in_proj",
    )(x2d, mod4, norm_w, w_in_b)
    return dict(zip([n for n, _ in SEG_WIDTHS], outs))


def _post_kernel(x_ref, ya_ref, yb_ref, yc_ref, yd_ref, mod_ref, nw_ref, wo_ref, wg_ref, wu_ref, wd_ref,
                 o_ref, x1_scr, h_scr, acc_scr):
    f = pl.program_id(1)

    @pl.when(f == 0)
    def _():
        mixed = _dot(ya_ref[...].astype(BF16), wo_ref[0:GROUP_W, :])
        mixed += _dot(yb_ref[...].astype(BF16), wo_ref[GROUP_W:2 * GROUP_W, :])
        mixed += _dot(yc_ref[...].astype(BF16), wo_ref[2 * GROUP_W:3 * GROUP_W, :])
        mixed += _dot(yd_ref[...].astype(BF16), wo_ref[3 * GROUP_W:4 * GROUP_W, :])
        x1 = x_ref[...] + (1.0 + mod_ref[2]) * _rms(mixed, nw_ref[0])
        x1_scr[...] = x1
        h = _rms(x1, nw_ref[1]) * (1.0 + mod_ref[4]) + mod_ref[3]
        h_scr[...] = h.astype(BF16)
        acc_scr[...] = jnp.zeros_like(acc_scr)

    hb = h_scr[...]
    a = _silu(_dot(hb, wg_ref[...])) * _dot(hb, wu_ref[...])
    acc_scr[...] += _dot(a.astype(BF16), wd_ref[...])

    @pl.when(f == pl.num_programs(1) - 1)
    def _():
        o_ref[...] = x1_scr[...] + (1.0 + mod_ref[5]) * _rms(acc_scr[...], nw_ref[2])


def _post(x2d, ys, mod4, norms3, w_out_b, w_gate_b, w_up_b, w_down_b, layer, tm, tf):
    n_tok = x2d.shape[0]
    nb, _, r, _ = mod4.shape
    tiles_per_mod = (n_tok // tm) // nb
    yspec = pl.BlockSpec((tm, GROUP_W), lambda i, f: (i, 0))
    return pl.pallas_call(
        _post_kernel,
        grid=(n_tok // tm, D_FF // tf),
        in_specs=[pl.BlockSpec((tm, D_MODEL), lambda i, f: (i, 0)), yspec, yspec, yspec, yspec,
                  pl.BlockSpec((None, N_MOD, r, D_MODEL), lambda i, f: (i // tiles_per_mod, 0, 0, 0)),
                  pl.BlockSpec((None, 3, 1, D_MODEL), lambda i, f: (layer, 0, 0, 0)),
                  pl.BlockSpec((None, D_MODEL, D_MODEL), lambda i, f: (layer, 0, 0)),
                  pl.BlockSpec((None, D_MODEL, tf), lambda i, f: (layer, 0, f)),
                  pl.BlockSpec((None, D_MODEL, tf), lambda i, f: (layer, 0, f)),
                  pl.BlockSpec((None, tf, D_MODEL), lambda i, f: (layer, f, 0))],
        out_specs=pl.BlockSpec((tm, D_MODEL), lambda i, f: (i, 0)),
        out_shape=jax.ShapeDtypeStruct((n_tok, D_MODEL), F32),
        scratch_shapes=[pltpu.VMEM((tm, D_MODEL), F32), pltpu.VMEM((tm, D_MODEL), BF16),
                        pltpu.VMEM((tm, D_MODEL), F32)],
        compiler_params=_cparams(("arbitrary", "arbitrary"), 56),
        name="post",
    )(x2d, ys[0], ys[1], ys[2], ys[3], mod4, norms3, w_out_b, w_gate_b, w_up_b, w_down_b)


def _ssd_kernel(xbc_ref, z_ref, dt_ref, cprev_ref, sprev_ref, cw_ref, cb_ref, hp_ref, nw_ref,
                y_ref, cnew_ref, snew_ref, buf, st, *, q):
    c = pl.program_id(1)

    @pl.when(c == 0)
    def _():
        buf[0:SUBLANES, :] = cprev_ref[...]
        st[...] = sprev_ref[...]

    xin = xbc_ref[...]
    buf[SUBLANES:SUBLANES + q, :] = xin
    conv = cb_ref[...] + cw_ref[3:4, :] * xin
    for i in range(SSM_CONV - 1):
        conv += cw_ref[i:i + 1, :] * buf[SUBLANES - 3 + i:SUBLANES - 3 + i + q, :]
    tail = buf[q:q + SUBLANES, :]
    buf[0:SUBLANES, :] = tail
    cnew_ref[...] = tail

    xc = _silu(conv)
    xs = xc[:, 0:GROUP_W]
    bm = xc[:, GROUP_W:GROUP_W + 2 * SSM_STATE]
    cm = xc[:, GROUP_W + 2 * SSM_STATE:]
    hp = hp_ref[...]
    dt = _softplus(dt_ref[...] + hp[0:1, :])
    da = dt * (-jnp.exp(hp[1:2, :]))
    rows = lax.broadcasted_iota(jnp.int32, (q, q), 0)
    cols = lax.broadcasted_iota(jnp.int32, (q, q), 1)
    tril = rows >= cols
    cs = _dot(tril.astype(F32), da, HI)
    onehot = (lax.broadcasted_iota(jnp.int32, (SUBLANES, LANES), 0)
              == lax.broadcasted_iota(jnp.int32, (SUBLANES, LANES), 1)).astype(F32)
    cs_t = _dot_nt(onehot, cs, HI)

    heads = range(N_HEADS)
    bgs = [bm[:, g * SSM_STATE:(g + 1) * SSM_STATE] for g in range(2)]
    cgs = [cm[:, g * SSM_STATE:(g + 1) * SSM_STATE] for g in range(2)]
    cbs = [_dot_nt(cgs[g], bgs[g]) for g in range(2)]
    cs_col = [cs[:, hd:hd + 1] for hd in heads]
    cs_last = [cs[q - 1:q, hd:hd + 1] for hd in heads]
    decay_in = [jnp.where(tril, jnp.exp(jnp.where(tril, cs_col[hd] - cs_t[hd:hd + 1, :], 0.0)), 0.0) for hd in heads]
    xh = [xs[:, hd * HEAD_DIM:(hd + 1) * HEAD_DIM] for hd in heads]
    xdt = [xh[hd] * dt[:, hd:hd + 1] for hd in heads]
    s_prev = [st[hd] for hd in heads]
    y_diag = [_dot(cbs[hd // 2] * decay_in[hd], xdt[hd]) for hd in heads]
    y_off = [_dot_nt(cgs[hd // 2], s_prev[hd]) for hd in heads]
    s_upd = [_dot_tn(xdt[hd] * jnp.exp(cs_last[hd] - cs_col[hd]), bgs[hd // 2]) for hd in heads]
    for hd in heads:
        st[hd] = jnp.exp(cs_last[hd]) * s_prev[hd] + s_upd[hd]
    y_heads = [y_diag[hd] + y_off[hd] * jnp.exp(cs_col[hd]) + hp[2:3, hd:hd + 1] * xh[hd] for hd in heads]
    y = jnp.concatenate(y_heads, axis=-1) * _silu(z_ref[...])
    y_ref[...] = _rms(y, nw_ref[...])

    @pl.when(c == pl.num_programs(1) - 1)
    def _():
        snew_ref[...] = st[...]


def _ssd_head_params(dt_bias, a_log, d_skip):
    rows = jnp.stack([dt_bias, a_log, d_skip]).astype(F32)
    return jnp.pad(rows, ((0, SUBLANES - 3), (0, LANES - N_HEADS)))


def _rwkv_vec_params(p):
    small = jnp.stack([p['w0'], p['a0'], p['k_k'], p['k_a'], p['r_k'].reshape(GROUP_W), p['ln_x_w'], p['ln_x_b']])
    return jnp.concatenate([p['mu_shift'][None, :], jnp.pad(small, ((0, 0), (0, RWKV_PROJ - GROUP_W)))], axis=0)


def _ssd(xbc, z, dt, conv_prev8, ssm_prev, conv_w, conv_b, head_params, norm_w, n_seq, seq_len):
    q = math.gcd(seq_len, SSM_CHUNK)
    nc = seq_len // q
    tok = lambda b, c: (b * nc + c, 0)
    fixed2 = lambda b, c: (0, 0)
    y, conv_new8, ssm_new = pl.pallas_call(
        functools.partial(_ssd_kernel, q=q),
        grid=(n_seq, nc),
        in_specs=[pl.BlockSpec((q, SSM_XBC), tok), pl.BlockSpec((q, GROUP_W), tok), pl.BlockSpec((q, LANES), tok),
                  pl.BlockSpec((None, SUBLANES, SSM_XBC), lambda b, c: (b, 0, 0)),
                  pl.BlockSpec((None, N_HEADS, HEAD_DIM, SSM_STATE), lambda b, c: (b, 0, 0, 0)),
                  pl.BlockSpec((SSM_CONV, SSM_XBC), fixed2), pl.BlockSpec((1, SSM_XBC), fixed2),
                  pl.BlockSpec((SUBLANES, LANES), fixed2), pl.BlockSpec((1, GROUP_W), fixed2)],
        out_specs=[pl.BlockSpec((q, GROUP_W), tok),
                   pl.BlockSpec((None, SUBLANES, SSM_XBC), lambda b, c: (b, 0, 0)),
                   pl.BlockSpec((None, N_HEADS, HEAD_DIM, SSM_STATE), lambda b, c: (b, 0, 0, 0))],
        out_shape=[jax.ShapeDtypeStruct((n_seq * seq_len, GROUP_W), F32),
                   jax.ShapeDtypeStruct((n_seq, SUBLANES, SSM_XBC), F32),
                   jax.ShapeDtypeStruct((n_seq, N_HEADS, HEAD_DIM, SSM_STATE), F32)],
        scratch_shapes=[pltpu.VMEM((q + SUBLANES, SSM_XBC), F32), pltpu.VMEM((N_HEADS, HEAD_DIM, SSM_STATE), F32)],
        compiler_params=_cparams(("arbitrary", "arbitrary"), 32),
        name="ssd",
    )(xbc, z, dt, conv_prev8, ssm_prev, conv_w, conv_b, head_params, norm_w)
    return y, conv_new8[:, SUBLANES - (SSM_CONV - 1):, :], ssm_new


def _rwkv_kernel(u_ref, shprev_ref, sprev_ref, vp_ref, w2_ref, a2_ref, g2_ref,
                 y_ref, shnew_ref, snew_ref, buf, st, *, c):
    ci = pl.program_id(1)

    @pl.when(ci == 0)
    def _():
        buf[0:SUBLANES, :] = jnp.broadcast_to(shprev_ref[...], (SUBLANES, RWKV_PROJ))
        st[...] = sprev_ref[...]

    u = u_ref[...]
    buf[SUBLANES:SUBLANES + c, :] = u
    u_prev = buf[SUBLANES - 1:SUBLANES - 1 + c, :]
    buf[0:SUBLANES, :] = buf[c:c + SUBLANES, :]
    shnew_ref[...] = u[c - 1:c, :]

    vp = vp_ref[...]
    us = u + vp[0:1, :] * (u_prev - u)
    gw = GROUP_W
    r = us[:, 0:gw]
    k = us[:, gw:2 * gw]
    v = us[:, 2 * gw:3 * gw]
    wd = us[:, 3 * gw:3 * gw + 64]
    ad = us[:, 3 * gw + 64:3 * gw + 128]
    gd = us[:, 3 * gw + 128:]
    w0, a0, k_k, k_a = vp[1:2, 0:gw], vp[2:3, 0:gw], vp[3:4, 0:gw], vp[4:5, 0:gw]
    r_k, ln_w, ln_b = vp[5:6, 0:gw], vp[6:7, 0:gw], vp[7:8, 0:gw]
    w = -_softplus(-(w0 + _dot(jnp.tanh(wd), w2_ref[...]))) - 0.5
    a = _sigmoid(a0 + _dot(ad, a2_ref[...]))
    g = _dot(_sigmoid(gd), g2_ref[...])
    kk_raw = k * k_k
    k2 = k * (1.0 + (a - 1.0) * k_a)
    lw = -jnp.exp(w)

    rows = lax.broadcasted_iota(jnp.int32, (c, c), 0)
    cols = lax.broadcasted_iota(jnp.int32, (c, c), 1)
    tril = rows >= cols
    stril = rows > cols
    cum = _dot(tril.astype(F32), lw, HI)
    e_in = jnp.exp(cum)
    e_out = jnp.exp(-cum)
    e_prev = jnp.exp(cum - lw)
    eye = (rows == cols).astype(F32)
    levels = int(math.log2(c))

    y_heads = []
    for h in range(N_HEADS):
        sl = slice(h * HEAD_DIM, (h + 1) * HEAD_DIM)
        kk = kk_raw[:, sl]
        kk = kk / jnp.maximum(jnp.sqrt(jnp.sum(kk * kk, axis=-1, keepdims=True)), 1e-12)
        rh, kh, vh, ah = r[:, sl], k2[:, sl], v[:, sl], a[:, sl]
        r_t = rh * e_in[:, sl]
        k_t = kh * e_out[:, sl]
        b_t = kk * ah * e_out[:, sl]
        kap_t = kk * e_prev[:, sl]
        aa = _dot_nt(jnp.concatenate([kap_t, r_t], axis=0), jnp.concatenate([k_t, b_t], axis=0))
        a_kk = jnp.where(stril, aa[0:c, 0:c], 0.0)
        a_kb = jnp.where(stril, aa[0:c, c:2 * c], 0.0)
        a_rk = jnp.where(tril, aa[c:2 * c, 0:c], 0.0)
        a_rb = jnp.where(tril, aa[c:2 * c, c:2 * c], 0.0)
        t_inv = eye - a_kb
        n_pow = a_kb
        for _ in range(levels - 1):
            n_pow = _dot(n_pow, n_pow)
            t_inv = t_inv + _dot(t_inv, n_pow)
        s_prev = st[h]
        wu = _dot(t_inv, jnp.concatenate([kap_t, _dot(a_kk, vh)], axis=-1))
        uu = _dot_nt(wu[:, 0:HEAD_DIM], s_prev) + wu[:, HEAD_DIM:]
        yh = _dot_nt(r_t, s_prev) + _dot(a_rk, vh) - _dot(a_rb, uu)
        upd = _dot_tn(jnp.concatenate([vh, uu], axis=0), jnp.concatenate([k_t, -b_t], axis=0))
        st[h] = (s_prev + upd) * e_in[c - 1:c, sl]
        mu = jnp.mean(yh, axis=-1, keepdims=True)
        var = jnp.mean((yh - mu) ** 2, axis=-1, keepdims=True)
        yn = (yh - mu) * lax.rsqrt(var + RWKV_LN_EPS)
        bonus = jnp.sum(rh * kh * r_k[:, sl], axis=-1, keepdims=True) * vh
        y_heads.append((yn, bonus))
    yn = jnp.concatenate([p[0] for p in y_heads], axis=-1)
    bonus = jnp.concatenate([p[1] for p in y_heads], axis=-1)
    y_ref[...] = (yn * ln_w + ln_b + bonus) * g

    @pl.when(ci == pl.num_programs(1) - 1)
    def _():
        snew_ref[...] = st[...]


def _cumsum_rows(x, seg):
    idx = lax.broadcasted_iota(jnp.int32, x.shape, 0) % seg
    s = 1
    while s < seg:
        x = x + jnp.where(idx >= s, pltpu.roll(x, s, 0), 0.0)
        s *= 2
    return x


def _rwkv_kernel2(u_ref, shprev_ref, sprev_ref, vp_ref, w2_ref, a2_ref, g2_ref,
                  y_ref, shnew_ref, snew_ref, buf, st, *, c, g_chunks):
    ci = pl.program_id(1)
    n = c * g_chunks

    @pl.when(ci == 0)
    def _():
        buf[0:SUBLANES, :] = jnp.broadcast_to(shprev_ref[...], (SUBLANES, RWKV_PROJ))
        st[...] = sprev_ref[...]

    u = u_ref[...]
    buf[SUBLANES:SUBLANES + n, :] = u
    u_prev = buf[SUBLANES - 1:SUBLANES - 1 + n, :]
    buf[0:SUBLANES, :] = buf[n:n + SUBLANES, :]
    shnew_ref[...] = u[n - 1:n, :]

    vp = vp_ref[...]
    us = u + vp[0:1, :] * (u_prev - u)
    gw = GROUP_W
    r = us[:, 0:gw]
    k = us[:, gw:2 * gw]
    v = us[:, 2 * gw:3 * gw]
    wd = us[:, 3 * gw:3 * gw + 64]
    ad = us[:, 3 * gw + 64:3 * gw + 128]
    gd = us[:, 3 * gw + 128:]
    w0, a0, k_k, k_a = vp[1:2, 0:gw], vp[2:3, 0:gw], vp[3:4, 0:gw], vp[4:5, 0:gw]
    r_k, ln_w, ln_b = vp[5:6, 0:gw], vp[6:7, 0:gw], vp[7:8, 0:gw]
    w = -_softplus(-(w0 + _dot(jnp.tanh(wd), w2_ref[...]))) - 0.5
    a = _sigmoid(a0 + _dot(ad, a2_ref[...]))
    gate = _dot(_sigmoid(gd), g2_ref[...])
    kk_raw = k * k_k
    k2 = k * (1.0 + (a - 1.0) * k_a)
    lw = -jnp.exp(w)
    cum = _cumsum_rows(lw, c)
    e_in = jnp.exp(cum)
    e_out = jnp.exp(-cum)
    e_prev = jnp.exp(cum - lw)

    rows = lax.broadcasted_iota(jnp.int32, (c, c), 0)
    cols = lax.broadcasted_iota(jnp.int32, (c, c), 1)
    tril = rows >= cols
    stril = rows > cols
    eye = (rows == cols).astype(F32)
    levels = int(math.log2(c))
    units = [(gi, h) for gi in range(g_chunks) for h in range(N_HEADS)]

    def part(x, gi, h):
        return x[gi * c:(gi + 1) * c, h * HEAD_DIM:(h + 1) * HEAD_DIM]

    kk_heads = []
    for h in range(N_HEADS):
        kkh = kk_raw[:, h * HEAD_DIM:(h + 1) * HEAD_DIM]
        kk_heads.append(kkh / jnp.maximum(jnp.sqrt(jnp.sum(kkh * kkh, axis=-1, keepdims=True)), 1e-12))
    kk = jnp.concatenate(kk_heads, axis=-1)
    r_t_all = r * e_in
    k_t_all = k2 * e_out
    b_t_all = kk * a * e_out
    kap_t_all = kk * e_prev

    r_t = {q: part(r_t_all, *q) for q in units}
    k_t = {q: part(k_t_all, *q) for q in units}
    b_t = {q: part(b_t_all, *q) for q in units}
    kap_t = {q: part(kap_t_all, *q) for q in units}
    vv = {q: part(v, *q) for q in units}
    aa = {q: _dot_nt(jnp.concatenate([kap_t[q], r_t[q]], axis=0), jnp.concatenate([k_t[q], b_t[q]], axis=0))
          for q in units}
    a_kb = {q: jnp.where(stril, aa[q][0:c, c:2 * c], 0.0) for q in units}
    a_v = {q: jnp.concatenate([jnp.where(stril, aa[q][0:c, 0:c], 0.0), jnp.where(tril, aa[q][c:2 * c, 0:c], 0.0)],
                              axis=0) for q in units}
    a_rb = {q: jnp.where(tril, aa[q][c:2 * c, c:2 * c], 0.0) for q in units}
    base = min(RWKV_INV_BASE, c)

    def same_block(size):
        return (rows // size) == (cols // size)

    n_pow = {q: jnp.where(same_block(base), a_kb[q], 0.0) for q in units}
    t_inv = {q: eye - n_pow[q] for q in units}
    for _ in range(int(math.log2(base)) - 1):
        n_pow = {q: _dot(n_pow[q], n_pow[q]) for q in units}
        t_inv = {q: t_inv[q] + _dot(t_inv[q], n_pow[q]) for q in units}
    size = base
    while size < c:
        off = same_block(2 * size) & jnp.logical_not(same_block(size))
        ct = {q: _dot(jnp.where(off, a_kb[q], 0.0), t_inv[q]) for q in units}
        t_inv = {q: t_inv[q] - _dot(t_inv[q], ct[q]) for q in units}
        size *= 2
    av = {q: _dot(a_v[q], vv[q]) for q in units}
    wu = {q: _dot(t_inv[q], jnp.concatenate([kap_t[q], av[q][0:c]], axis=-1)) for q in units}
    arw = {q: _dot(a_rb[q], wu[q]) for q in units}
    r_hat = {q: r_t[q] - arw[q][:, 0:HEAD_DIM] for q in units}
    y_loc = {q: av[q][c:2 * c] - arw[q][:, HEAD_DIM:] for q in units}
    g_mat = {q: _dot_tn(wu[q][:, 0:HEAD_DIM], b_t[q]) for q in units}
    h_loc = {q: _dot_tn(jnp.concatenate([vv[q], wu[q][:, HEAD_DIM:]], axis=0),
                        jnp.concatenate([k_t[q], -b_t[q]], axis=0)) for q in units}

    y_heads = []
    for h in range(N_HEADS):
        s_cur = st[h]
        ys = []
        for gi in range(g_chunks):
            q = (gi, h)
            ys.append(_dot_nt(r_hat[q], s_cur) + y_loc[q])
            decay_end = e_in[(gi + 1) * c - 1:(gi + 1) * c, h * HEAD_DIM:(h + 1) * HEAD_DIM]
            s_cur = (s_cur - _dot(s_cur, g_mat[q]) + h_loc[q]) * decay_end
        st[h] = s_cur
        yh = jnp.concatenate(ys, axis=0) if g_chunks > 1 else ys[0]
        mu = jnp.mean(yh, axis=-1, keepdims=True)
        var = jnp.mean((yh - mu) ** 2, axis=-1, keepdims=True)
        y_heads.append((yh - mu) * lax.rsqrt(var + RWKV_LN_EPS))
    yn = jnp.concatenate(y_heads, axis=-1)
    rk = r * k2 * r_k
    bonus = jnp.concatenate(
        [jnp.sum(rk[:, h * HEAD_DIM:(h + 1) * HEAD_DIM], axis=-1, keepdims=True) * v[:, h * HEAD_DIM:(h + 1) * HEAD_DIM]
         for h in range(N_HEADS)], axis=-1)
    y_ref[...] = (yn * ln_w + ln_b + bonus) * gate

    @pl.when(ci == pl.num_programs(1) - 1)
    def _():
        snew_ref[...] = st[...]


RWKV_CHUNKS_PER_STEP = 4
RWKV_INV_BASE = 16


def _rwkv(pd, shift_prev, wkv_prev, vec_params, w2, a2, g2, n_seq, seq_len):
    c = math.gcd(seq_len, RWKV_CHUNK)
    g_chunks = math.gcd(seq_len // c, RWKV_CHUNKS_PER_STEP)
    n = c * g_chunks
    nc = seq_len // n
    tok = lambda b, i: (b * nc + i, 0)
    fixed2 = lambda b, i: (0, 0)
    y, shift_new, wkv_new = pl.pallas_call(
        functools.partial(_rwkv_kernel2, c=c, g_chunks=g_chunks),
        grid=(n_seq, nc),
        in_specs=[pl.BlockSpec((n, RWKV_PROJ), tok),
                  pl.BlockSpec((None, 1, RWKV_PROJ), lambda b, i: (b, 0, 0)),
                  pl.BlockSpec((None, N_HEADS, HEAD_DIM, HEAD_DIM), lambda b, i: (b, 0, 0, 0)),
                  pl.BlockSpec((SUBLANES, RWKV_PROJ), fixed2),
                  pl.BlockSpec((64, GROUP_W), fixed2), pl.BlockSpec((64, GROUP_W), fixed2),
                  pl.BlockSpec((128, GROUP_W), fixed2)],
        out_specs=[pl.BlockSpec((n, GROUP_W), tok),
                   pl.BlockSpec((None, 1, RWKV_PROJ), lambda b, i: (b, 0, 0)),
                   pl.BlockSpec((None, N_HEADS, HEAD_DIM, HEAD_DIM), lambda b, i: (b, 0, 0, 0))],
        out_shape=[jax.ShapeDtypeStruct((n_seq * seq_len, GROUP_W), F32),
                   jax.ShapeDtypeStruct((n_seq, 1, RWKV_PROJ), F32),
                   jax.ShapeDtypeStruct((n_seq, N_HEADS, HEAD_DIM, HEAD_DIM), F32)],
        scratch_shapes=[pltpu.VMEM((n + SUBLANES, RWKV_PROJ), F32), pltpu.VMEM((N_HEADS, HEAD_DIM, HEAD_DIM), F32)],
        compiler_params=_cparams(("arbitrary", "arbitrary"), 32),
        name="rwkv",
    )(pd, shift_prev, wkv_prev, vec_params, w2, a2, g2)
    return y, shift_new[:, 0, :], wkv_new


def _rope_tables(pos, unit):
    rot = unit // 4
    half = rot // 2
    inv_freq = ROPE_THETA ** (-jnp.arange(half, dtype=jnp.float32) / half)
    ang = pos.astype(jnp.float32)[:, None] * inv_freq[None, :]
    cos, sin = jnp.cos(ang), jnp.sin(ang)
    t = pos.shape[0]
    zero_h = jnp.zeros((t, half), F32)
    rest0 = jnp.zeros((t, unit - rot), F32)
    c = jnp.concatenate([cos, cos, jnp.ones((t, unit - rot), F32)], axis=-1)
    sa = jnp.concatenate([-sin, zero_h, rest0], axis=-1)
    sb = jnp.concatenate([zero_h, sin, rest0], axis=-1)
    reps = GROUP_W // unit
    return tuple(jnp.tile(a, (1, reps)) for a in (c, sa, sb))


def _rope(x, c, sa, sb, half):
    n = x.shape[-1]
    return x * c + pltpu.roll(x, n - half, 1) * sa + pltpu.roll(x, half, 1) * sb


def _moba_prompt_kernel(q_ref, k_ref, v_ref, c_ref, sa_ref, sb_ref, y_ref, kout_ref, km_scr, *, nb):
    i = pl.program_id(1)
    blk = MOBA_BLOCK
    half = HEAD_DIM // 8

    @pl.when(i == 0)
    def _():
        kr = _rope(k_ref[...], c_ref[...], sa_ref[...], sb_ref[...], half)
        kout_ref[...] = kr
        km_scr[...] = jnp.zeros_like(km_scr)
        for n in range(nb):
            km_scr[n:n + 1, :] = jnp.mean(kr[n * blk:(n + 1) * blk, :], axis=0, keepdims=True)

    start = pl.multiple_of(i * blk, blk)
    q = _rope(q_ref[...], c_ref[pl.ds(start, blk), :], sa_ref[pl.ds(start, blk), :], sb_ref[pl.ds(start, blk), :], half)
    scale = HEAD_DIM ** -0.5
    lane = lax.broadcasted_iota(jnp.int32, (blk, LANES), 1)
    rows = lax.broadcasted_iota(jnp.int32, (blk, blk), 0)
    cols = lax.broadcasted_iota(jnp.int32, (blk, blk), 1)
    outs = []
    for h in range(N_HEADS):
        sl = slice(h * HEAD_DIM, (h + 1) * HEAD_DIM)
        qh = q[:, sl]
        gate = jnp.where(lane < i, _dot_nt(qh, km_scr[:, sl], HI), -jnp.inf)
        cnt = jnp.zeros((blk, LANES), F32)
        for m in range(nb):
            gm = gate[:, m:m + 1]
            cnt += jnp.where((gm > gate) | ((gm == gate) & (m < lane)), 1.0, 0.0)
        sel = jnp.where((lane < i) & (cnt < MOBA_TOPK), 1.0, 0.0)

        s = _dot_nt(qh, kout_ref[pl.ds(start, blk), :][:, sl]) * scale
        s = jnp.where(cols <= rows, s, NEG_BIG)
        m0 = jnp.max(s, axis=-1, keepdims=True)
        p = jnp.exp(s - m0)
        l0 = jnp.sum(p, axis=-1, keepdims=True)
        acc0 = _dot(p, v_ref[pl.ds(start, blk), :][:, sl])

        def body(n, carry, qh=qh, sel=sel, sl=sl):
            m_run, l_run, acc = carry
            off = pl.multiple_of(n * blk, blk)
            s = _dot_nt(qh, kout_ref[pl.ds(off, blk), :][:, sl]) * scale
            chosen = jnp.sum(jnp.where(lane == n, sel, 0.0), axis=-1, keepdims=True) > 0.5
            s = jnp.where(chosen, s, NEG_BIG)
            m_new = jnp.maximum(m_run, jnp.max(s, axis=-1, keepdims=True))
            alpha = jnp.exp(m_run - m_new)
            p = jnp.exp(s - m_new)
            l_new = alpha * l_run + jnp.sum(p, axis=-1, keepdims=True)
            acc = alpha * acc + _dot(p, v_ref[pl.ds(off, blk), :][:, sl])
            return m_new, l_new, acc

        _, l_fin, acc_fin = lax.fori_loop(0, i, body, (m0, l0, acc0))
        outs.append(acc_fin / l_fin)
    y_ref[...] = jnp.concatenate(outs, axis=-1)


def _moba_prompt(q, k, v, tabs, n_seq, seq_len):
    nb = seq_len // MOBA_BLOCK
    seq = lambda b, i: (b, 0)
    fixed = lambda b, i: (0, 0)
    full = pl.BlockSpec((seq_len, GROUP_W), seq)
    tab = pl.BlockSpec((seq_len, GROUP_W), fixed)
    y, k_rot = pl.pallas_call(
        functools.partial(_moba_prompt_kernel, nb=nb),
        grid=(n_seq, nb),
        in_specs=[pl.BlockSpec((MOBA_BLOCK, GROUP_W), lambda b, i: (b * nb + i, 0)), full, full, tab, tab, tab],
        out_specs=[pl.BlockSpec((MOBA_BLOCK, GROUP_W), lambda b, i: (b * nb + i, 0)), full],
        out_shape=[jax.ShapeDtypeStruct((n_seq * seq_len, GROUP_W), F32)] * 2,
        scratch_shapes=[pltpu.VMEM((LANES, GROUP_W), F32)],
        compiler_params=_cparams(("arbitrary", "arbitrary"), 48),
        name="moba_prompt",
    )(q, k, v, *tabs)
    return y, k_rot


def _diff_params(p):
    rows = [jnp.pad(p[n], (0, LANES - DIFF_DH)) for n in ('lambda_q1', 'lambda_k1', 'lambda_q2', 'lambda_k2')]
    rows.append(jnp.pad(p['subln_w'], (0, LANES - 2 * DIFF_DH)))
    return jnp.pad(jnp.stack(rows), ((0, SUBLANES - 5), (0, 0)))


def _diff_lambda(dp, lam_init):
    return (jnp.exp(jnp.sum(dp[0:1, :] * dp[1:2, :], axis=-1, keepdims=True))
            - jnp.exp(jnp.sum(dp[2:3, :] * dp[3:4, :], axis=-1, keepdims=True)) + lam_init)


def _diff_prompt_kernel(q_ref, k_ref, v_ref, c_ref, sa_ref, sb_ref, dp_ref, y_ref, kout_ref, *, lam_init, tq):
    i = pl.program_id(1)
    half = DIFF_DH // 8

    @pl.when(i == 0)
    def _():
        kout_ref[...] = _rope(k_ref[...], c_ref[...], sa_ref[...], sb_ref[...], half)

    start = pl.multiple_of(i * tq, tq)
    q = _rope(q_ref[...], c_ref[pl.ds(start, tq), :], sa_ref[pl.ds(start, tq), :], sb_ref[pl.ds(start, tq), :], half)
    dp = dp_ref[...]
    lam = _diff_lambda(dp, lam_init)
    subln = dp[4:5, 0:2 * DIFF_DH]
    scale = DIFF_DH ** -0.5
    rows = lax.broadcasted_iota(jnp.int32, (tq, tq), 0)
    cols = lax.broadcasted_iota(jnp.int32, (tq, tq), 1)
    outs = []
    for h in range(N_HEADS):
        sl = slice(h * HEAD_DIM, (h + 1) * HEAD_DIM)
        qs = (q[:, h * HEAD_DIM:h * HEAD_DIM + DIFF_DH], q[:, h * HEAD_DIM + DIFF_DH:(h + 1) * HEAD_DIM])

        def step(off, causal, carry, qs=qs, sl=sl):
            kb = kout_ref[pl.ds(off, tq), :][:, sl]
            vb = v_ref[pl.ds(off, tq), :][:, sl]
            new = []
            for j in range(2):
                s = _dot_nt(qs[j], kb[:, j * DIFF_DH:(j + 1) * DIFF_DH]) * scale
                if causal:
                    s = jnp.where(cols <= rows, s, NEG_BIG)
                    m_new = jnp.max(s, axis=-1, keepdims=True)
                    p = jnp.exp(s - m_new)
                    new += [m_new, jnp.sum(p, axis=-1, keepdims=True), _dot(p, vb)]
                else:
                    m_run, l_run, acc = carry[3 * j:3 * j + 3]
                    m_new = jnp.maximum(m_run, jnp.max(s, axis=-1, keepdims=True))
                    alpha = jnp.exp(m_run - m_new)
                    p = jnp.exp(s - m_new)
                    new += [m_new, alpha * l_run + jnp.sum(p, axis=-1, keepdims=True), alpha * acc + _dot(p, vb)]
            return tuple(new)

        init = step(start, True, None)
        fin = lax.fori_loop(0, i, lambda n, carry: step(pl.multiple_of(n * tq, tq), False, carry), init)
        o = fin[2] / fin[1] - lam * (fin[5] / fin[4])
        outs.append(_rms(o, subln) * (1.0 - lam_init))
    y_ref[...] = jnp.concatenate(outs, axis=-1)


def _diff_prompt(q, k, v, tabs, dparams, lam_init, n_seq, seq_len):
    tq = 256
    nt = seq_len // tq
    seq = lambda b, i: (b, 0)
    fixed = lambda b, i: (0, 0)
    full = pl.BlockSpec((seq_len, GROUP_W), seq)
    tab = pl.BlockSpec((seq_len, GROUP_W), fixed)
    y, k_rows = pl.pallas_call(
        functools.partial(_diff_prompt_kernel, lam_init=float(lam_init), tq=tq),
        grid=(n_seq, nt),
        in_specs=[pl.BlockSpec((tq, GROUP_W), lambda b, i: (b * nt + i, 0)), full, full, tab, tab, tab,
                  pl.BlockSpec((SUBLANES, LANES), fixed)],
        out_specs=[pl.BlockSpec((tq, GROUP_W), lambda b, i: (b * nt + i, 0)), full],
        out_shape=[jax.ShapeDtypeStruct((n_seq * seq_len, GROUP_W), F32)] * 2,
        compiler_params=_cparams(("arbitrary", "arbitrary"), 48),
        name="diff_prompt",
    )(q, k, v, *tabs, dparams)
    return y, k_rows


ATT_TILE = 256
ATT_KEYS = 256
ATT_SUB = ATT_TILE // ATT_KEYS


def _prep_kv(k_ref, v_ref, tabs, half, kout_ref, k_scr, vt_scr):
    c_ref, sa_ref, sb_ref = tabs
    kr = _rope(k_ref[...], c_ref[...], sa_ref[...], sb_ref[...], half)
    kout_ref[...] = kr
    for h in range(N_HEADS):
        k_scr[h] = kr[:, h * HEAD_DIM:(h + 1) * HEAD_DIM].astype(BF16)
    vt_scr[...] = v_ref[...].T.astype(BF16)
    return kr


def _kv_tiles(k_scr, vt_scr, h, off):
    return k_scr[h, pl.ds(off, ATT_KEYS), :], vt_scr[h * HEAD_DIM:(h + 1) * HEAD_DIM, pl.ds(off, ATT_KEYS)]


def _attend_step(k_scr, vt_scr, chains, off, carries, keeps):
    tiles = {h: _kv_tiles(k_scr, vt_scr, h, off) for h in {h for h, _ in chains}}
    scores = [_dot(tiles[h][0], qs) for h, qs in chains]
    stats = []
    for s, carry, keep in zip(scores, carries, keeps):
        if keep is not None:
            s = jnp.where(keep, s, NEG_BIG)
        m_new = jnp.max(s, axis=0, keepdims=True)
        alpha = None
        if carry is not None:
            m_new = jnp.maximum(carry[0], m_new)
            alpha = jnp.exp(carry[0] - m_new)
        p = jnp.exp(s - m_new)
        l_new = jnp.sum(p, axis=0, keepdims=True)
        if carry is not None:
            l_new = alpha * carry[1] + l_new
        stats.append((m_new, l_new, alpha, p.astype(BF16)))
    pvs = [_dot(tiles[h][1], st[3]) for (h, _), st in zip(chains, stats)]
    return tuple((st[0], st[1], pv if st[2] is None else st[2] * carry[2] + pv)
                 for st, pv, carry in zip(stats, pvs, carries))


def _own_block(k_scr, vt_scr, chains, start):
    key_i = lax.broadcasted_iota(jnp.int32, (ATT_KEYS, ATT_TILE), 0)
    qry_i = lax.broadcasted_iota(jnp.int32, (ATT_KEYS, ATT_TILE), 1)
    carries = [None] * len(chains)
    for r in range(ATT_SUB):
        keep = key_i + r * ATT_KEYS <= qry_i
        carries = _attend_step(k_scr, vt_scr, chains, start + r * ATT_KEYS, carries, [keep] * len(chains))
    return carries


def _moba_prompt_kernel2(q_ref, k_ref, v_ref, c_ref, sa_ref, sb_ref, y_ref, kout_ref, k_scr, vt_scr, km_scr, *, nb):
    i = pl.program_id(1)
    tq = ATT_TILE
    half = HEAD_DIM // 8
    nbp = km_scr.shape[0]

    @pl.when(i == 0)
    def _():
        kr = _prep_kv(k_ref, v_ref, (c_ref, sa_ref, sb_ref), half, kout_ref, k_scr, vt_scr)
        km_scr[...] = jnp.zeros_like(km_scr)
        for n in range(nb):
            km_scr[n:n + 1, :] = jnp.mean(kr[n * tq:(n + 1) * tq, :], axis=0, keepdims=True)

    start = pl.multiple_of(i * tq, tq)
    q = _rope(q_ref[...], c_ref[pl.ds(start, tq), :], sa_ref[pl.ds(start, tq), :], sb_ref[pl.ds(start, tq), :], half)
    qt = q.T
    scale = HEAD_DIM ** -0.5
    blk_i = lax.broadcasted_iota(jnp.int32, (nbp, tq), 0)
    valid = blk_i < i

    chains, sels = [], []
    for h in range(N_HEADS):
        qth = qt[h * HEAD_DIM:(h + 1) * HEAD_DIM, :]
        gate = jnp.where(valid, _dot(km_scr[:, h * HEAD_DIM:(h + 1) * HEAD_DIM], qth, HI), -jnp.inf)
        cnt = jnp.zeros((nbp, tq), F32)
        for m in range(nb):
            gm = gate[m:m + 1, :]
            cnt += jnp.where((gm > gate) | ((gm == gate) & (m < blk_i)), 1.0, 0.0)
        sels.append(jnp.where(valid & (cnt < MOBA_TOPK), 1.0, 0.0))
        chains.append((h, (qth * scale).astype(BF16)))

    def body(n, carries):
        off = pl.multiple_of(n * ATT_KEYS, ATT_KEYS)
        chosen = [jnp.sum(jnp.where(blk_i == n // ATT_SUB, sel, 0.0), axis=0, keepdims=True) > 0.5 for sel in sels]
        return _attend_step(k_scr, vt_scr, chains, off, carries, chosen)

    fin = lax.fori_loop(0, i * ATT_SUB, body, tuple(_own_block(k_scr, vt_scr, chains, start)))
    yt = jnp.concatenate([acc / l_run for (_, l_run, acc) in fin], axis=0)
    y_ref[...] = yt.T


def _moba_prompt2(q, k, v, tabs, n_seq, seq_len):
    nb = seq_len // ATT_TILE
    nbp = -(-nb // SUBLANES) * SUBLANES
    seq = lambda b, i: (b, 0)
    fixed = lambda b, i: (0, 0)
    full = pl.BlockSpec((seq_len, GROUP_W), seq)
    tab = pl.BlockSpec((seq_len, GROUP_W), fixed)
    tile = pl.BlockSpec((ATT_TILE, GROUP_W), lambda b, i: (b * nb + i, 0))
    y, k_rot = pl.pallas_call(
        functools.partial(_moba_prompt_kernel2, nb=nb),
        grid=(n_seq, nb),
        in_specs=[tile, full, full, tab, tab, tab],
        out_specs=[tile, full],
        out_shape=[jax.ShapeDtypeStruct((n_seq * seq_len, GROUP_W), F32)] * 2,
        scratch_shapes=[pltpu.VMEM((N_HEADS, seq_len, HEAD_DIM), BF16), pltpu.VMEM((GROUP_W, seq_len), BF16),
                        pltpu.VMEM((nbp, GROUP_W), F32)],
        compiler_params=_cparams(("arbitrary", "arbitrary"), 56),
        name="moba_prompt",
    )(q, k, v, *tabs)
    return y, k_rot


def _diff_prompt_kernel2(q_ref, k_ref, v_ref, c_ref, sa_ref, sb_ref, dp_ref, sw_ref, y_ref, kout_ref, k_scr, vt_scr,
                         *, lam_init):
    i = pl.program_id(1)
    tq = ATT_TILE
    half = DIFF_DH // 8

    @pl.when(i == 0)
    def _():
        _prep_kv(k_ref, v_ref, (c_ref, sa_ref, sb_ref), half, kout_ref, k_scr, vt_scr)

    start = pl.multiple_of(i * tq, tq)
    q = _rope(q_ref[...], c_ref[pl.ds(start, tq), :], sa_ref[pl.ds(start, tq), :], sb_ref[pl.ds(start, tq), :], half)
    qt = q.T
    scale = DIFF_DH ** -0.5
    dim_i = lax.broadcasted_iota(jnp.int32, (HEAD_DIM, tq), 0)

    chains = []
    for h in range(N_HEADS):
        qth = qt[h * HEAD_DIM:(h + 1) * HEAD_DIM, :] * scale
        for j in range(2):
            chains.append((h, jnp.where((dim_i >= DIFF_DH) == (j == 1), qth, 0.0).astype(BF16)))

    def body(n, carries):
        off = pl.multiple_of(n * ATT_KEYS, ATT_KEYS)
        return _attend_step(k_scr, vt_scr, chains, off, carries, [None] * len(chains))

    fin = lax.fori_loop(0, i * ATT_SUB, body, tuple(_own_block(k_scr, vt_scr, chains, start)))
    lam = _diff_lambda(dp_ref[...], lam_init)
    outs = []
    for h in range(N_HEADS):
        (_, l1, a1), (_, l2, a2) = fin[2 * h], fin[2 * h + 1]
        o = a1 / l1 - lam * (a2 / l2)
        o = o * lax.rsqrt(jnp.mean(o * o, axis=0, keepdims=True) + RMS_EPS) * sw_ref[...]
        outs.append(o * (1.0 - lam_init))
    y_ref[...] = jnp.concatenate(outs, axis=0).T


def _diff_prompt2(q, k, v, tabs, dparams, subln_w, lam_init, n_seq, seq_len):
    nt = seq_len // ATT_TILE
    seq = lambda b, i: (b, 0)
    fixed = lambda b, i: (0, 0)
    full = pl.BlockSpec((seq_len, GROUP_W), seq)
    tab = pl.BlockSpec((seq_len, GROUP_W), fixed)
    tile = pl.BlockSpec((ATT_TILE, GROUP_W), lambda b, i: (b * nt + i, 0))
    subln_cols = jnp.broadcast_to(subln_w[:, None], (HEAD_DIM, ATT_TILE))
    y, k_rows = pl.pallas_call(
        functools.partial(_diff_prompt_kernel2, lam_init=float(lam_init)),
        grid=(n_seq, nt),
        in_specs=[tile, full, full, tab, tab, tab, pl.BlockSpec((SUBLANES, LANES), fixed),
                  pl.BlockSpec((HEAD_DIM, ATT_TILE), fixed)],
        out_specs=[tile, full],
        out_shape=[jax.ShapeDtypeStruct((n_seq * seq_len, GROUP_W), F32)] * 2,
        scratch_shapes=[pltpu.VMEM((N_HEADS, seq_len, HEAD_DIM), BF16), pltpu.VMEM((GROUP_W, seq_len), BF16)],
        compiler_params=_cparams(("arbitrary", "arbitrary"), 56),
        name="diff_prompt",
    )(q, k, v, *tabs, dparams, subln_cols)
    return y, k_rows


SAMPLE_T = 8
NQ_MOBA = N_HEADS * SAMPLE_T
NQ_DIFF = 2 * N_HEADS * SAMPLE_T


def _block_diag_queries(q, n_maps):
    nq = n_maps * N_HEADS * SAMPLE_T
    tiled = jnp.concatenate([q] * (nq // SAMPLE_T) + [jnp.zeros((LANES - nq, GROUP_W), F32)], axis=0)
    row = lax.broadcasted_iota(jnp.int32, (LANES, GROUP_W), 0)
    lane = lax.broadcasted_iota(jnp.int32, (LANES, GROUP_W), 1)
    width = HEAD_DIM // n_maps
    keep = (row < nq) & ((row // SAMPLE_T) % N_HEADS == lane // HEAD_DIM) & (row // (N_HEADS * SAMPLE_T) == (lane % HEAD_DIM) // width)
    return jnp.where(keep, tiled, 0.0)


def _block_partials(kb, vb, qblk, scale, nq, mask=None):
    s = _dot_nt(kb, qblk) * scale
    if mask is not None:
        s = jnp.where(mask, s, NEG_BIG)
    m = jnp.max(s, axis=0, keepdims=True)
    p = jnp.exp(s - m)
    l = jnp.sum(p, axis=0, keepdims=True)
    return m, l, _dot_tn(p[:, 0:nq], vb)


def _merge_blocks(m_scr, l_scr, o_scr, w_scr, sel, own, nblk, nq):
    m_own, l_own, o_own = own
    m_all = m_scr[0:nblk, :]
    m_tot = jnp.maximum(jnp.max(jnp.where(sel, m_all, NEG_BIG), axis=0, keepdims=True), m_own)
    w = jnp.where(sel, jnp.exp(m_all - m_tot), 0.0)
    e_own = jnp.exp(m_own - m_tot)
    den = jnp.sum(w * l_scr[0:nblk, :], axis=0, keepdims=True) + e_own * l_own
    w_scr[...] = jnp.zeros_like(w_scr)
    w_scr[0:nblk, :] = w
    w_scr[nblk:nblk + 1, :] = e_own
    w_scr[nblk + 1:nblk + 2, :] = den
    wt = w_scr[...].T
    num = wt[0:nq, nblk:nblk + 1] * o_own
    for n in range(nblk):
        num += wt[0:nq, n:n + 1] * o_scr[n]
    return num / wt[0:nq, nblk + 1:nblk + 2]


def _sample_attn_kernel(pt_ref, mq_ref, mk_ref, mv_ref, dq_ref, dk_ref, dv_ref,
                        mka_ref, mkb_ref, mva_ref, mvb_ref, dka_ref, dkb_ref, dva_ref, dvb_ref,
                        mc_ref, msa_ref, msb_ref, dc_ref, dsa_ref, dsb_ref, dp_ref,
                        ym_ref, yd_ref, mkout_ref, dkout_ref,
                        qm_scr, qd_scr, g_scr, mm_scr, lm_scr, om_scr, md_scr, ld_scr, od_scr, w_scr,
                        *, nblk, lam_init):
    n = pl.program_id(1)
    m_half = HEAD_DIM // 8
    d_half = DIFF_DH // 8
    m_scale = HEAD_DIM ** -0.5
    d_scale = DIFF_DH ** -0.5

    @pl.when(n == 0)
    def _():
        qm = _rope(mq_ref[...], mc_ref[...], msa_ref[...], msb_ref[...], m_half)
        qm_scr[...] = _block_diag_queries(qm, 1)
        qd = _rope(dq_ref[...], dc_ref[...], dsa_ref[...], dsb_ref[...], d_half)
        qd_scr[...] = _block_diag_queries(qd, 2)

    kb = jnp.concatenate([mka_ref[...], mkb_ref[...]], axis=0)
    vb = jnp.concatenate([mva_ref[...], mvb_ref[...]], axis=0)
    qm_blk = qm_scr[...]
    m, l, o = _block_partials(kb, vb, qm_blk, m_scale, NQ_MOBA)
    mm_scr[pl.ds(n, 1), :] = m
    lm_scr[pl.ds(n, 1), :] = l
    om_scr[n] = o
    k_mean = jnp.broadcast_to(jnp.mean(kb, axis=0, keepdims=True), (SUBLANES, GROUP_W))
    g_scr[pl.ds(n, 1), :] = _dot_nt(k_mean, qm_blk, HI)[0:1, :]

    kb = jnp.concatenate([dka_ref[...], dkb_ref[...]], axis=0)
    vb = jnp.concatenate([dva_ref[...], dvb_ref[...]], axis=0)
    m, l, o = _block_partials(kb, vb, qd_scr[...], d_scale, NQ_DIFF)
    md_scr[pl.ds(n, 1), :] = m
    ld_scr[pl.ds(n, 1), :] = l
    od_scr[n] = o

    @pl.when(n == nblk - 1)
    def _():
        key = lax.broadcasted_iota(jnp.int32, (SAMPLE_T, LANES), 0)
        qry = lax.broadcasted_iota(jnp.int32, (SAMPLE_T, LANES), 1) % SAMPLE_T
        causal = key <= qry
        blk_row = lax.broadcasted_iota(jnp.int32, (nblk, LANES), 0)

        k_new = _rope(mk_ref[...], mc_ref[...], msa_ref[...], msb_ref[...], m_half)
        mkout_ref[...] = k_new
        own = _block_partials(k_new, mv_ref[...], qm_scr[...], m_scale, NQ_MOBA, causal)
        gate = g_scr[0:nblk, :]
        cnt = jnp.zeros((nblk, LANES), F32)
        for j in range(nblk):
            gj = gate[j:j + 1, :]
            cnt += jnp.where((gj > gate) | ((gj == gate) & (j < blk_row)), 1.0, 0.0)
        rows_m = _merge_blocks(mm_scr, lm_scr, om_scr, w_scr, cnt < MOBA_TOPK, own, nblk, NQ_MOBA)
        ym_ref[...] = jnp.concatenate(
            [rows_m[h * SAMPLE_T:(h + 1) * SAMPLE_T, h * HEAD_DIM:(h + 1) * HEAD_DIM] for h in range(N_HEADS)], axis=-1)

        k_new = _rope(dk_ref[...], dc_ref[...], dsa_ref[...], dsb_ref[...], d_half)
        dkout_ref[...] = k_new
        own = _block_partials(k_new, dv_ref[...], qd_scr[...], d_scale, NQ_DIFF, causal)
        rows_d = _merge_blocks(md_scr, ld_scr, od_scr, w_scr, blk_row >= 0, own, nblk, NQ_DIFF)
        dp = dp_ref[...]
        lam = _diff_lambda(dp, lam_init)
        subln = dp[4:5, 0:2 * DIFF_DH]
        outs = []
        for h in range(N_HEADS):
            o1 = rows_d[h * SAMPLE_T:(h + 1) * SAMPLE_T, h * HEAD_DIM:(h + 1) * HEAD_DIM]
            o2 = rows_d[NQ_MOBA + h * SAMPLE_T:NQ_MOBA + (h + 1) * SAMPLE_T, h * HEAD_DIM:(h + 1) * HEAD_DIM]
            outs.append(_rms(o1 - lam * o2, subln) * (1.0 - lam_init))
        yd_ref[...] = jnp.concatenate(outs, axis=-1)


def _sample_attn(mq, mk, mv, dq, dk, dv, caches, page_table, tabs_m, tabs_d, dparams, lam_init, layer, n_seq):
    n_pages = page_table.shape[1]
    page = caches[0].shape[2]
    per_blk = MOBA_BLOCK // page
    nblk = n_pages // per_blk
    depth, n_pool = caches[0].shape[:2]
    pools = [c.reshape(depth, n_pool, page, GROUP_W) for c in caches]
    new = pl.BlockSpec((SAMPLE_T, GROUP_W), lambda b, n, pt: (b, 0))
    tab = pl.BlockSpec((SAMPLE_T, GROUP_W), lambda b, n, pt: (0, 0))

    def page_spec(j):
        return pl.BlockSpec((None, None, page, GROUP_W), lambda b, n, pt: (layer, pt[b, per_blk * n + j], 0, 0))

    cache_specs, cache_args = [], []
    for pool in pools:
        for j in range(per_blk):
            cache_specs.append(page_spec(j))
            cache_args.append(pool)
    out_sd = jax.ShapeDtypeStruct((n_seq * SAMPLE_T, GROUP_W), F32)
    stat = pltpu.VMEM((LANES, LANES), F32)
    grid_spec = pltpu.PrefetchScalarGridSpec(
        num_scalar_prefetch=1,
        grid=(n_seq, nblk),
        in_specs=[new] * 6 + cache_specs + [tab] * 6 + [pl.BlockSpec((SUBLANES, LANES), lambda b, n, pt: (0, 0))],
        out_specs=[new] * 4,
        scratch_shapes=[pltpu.VMEM((LANES, GROUP_W), F32), pltpu.VMEM((LANES, GROUP_W), F32),
                        stat, stat, stat, pltpu.VMEM((nblk, NQ_MOBA, GROUP_W), F32),
                        stat, stat, pltpu.VMEM((nblk, NQ_DIFF, GROUP_W), F32), stat],
    )
    return pl.pallas_call(
        functools.partial(_sample_attn_kernel, nblk=nblk, lam_init=float(lam_init)),
        grid_spec=grid_spec,
        out_shape=[out_sd] * 4,
        compiler_params=_cparams(("arbitrary", "arbitrary"), 32),
        name="sample_attn",
    )(page_table, mq, mk, mv, dq, dk, dv, *cache_args, *tabs_m, *tabs_d, dparams)


def _page_partials(kts, vts, qblk, scale, nq):
    qb = qblk[0:nq, :].astype(BF16)
    ss = [_dot(qb, kt.astype(BF16)) * scale for kt in kts]
    m = ss[0].max(axis=-1, keepdims=True)
    for s in ss[1:]:
        m = jnp.maximum(m, s.max(axis=-1, keepdims=True))
    ps = [jnp.exp(s - m) for s in ss]
    l = sum(p.sum(axis=-1, keepdims=True) for p in ps)
    o = sum(_dot_nt(p.astype(BF16), vt.astype(BF16)) for p, vt in zip(ps, vts))
    return m, l, o


def _own_partials(k_new, v_new, qblk, scale, nq):
    pad = jnp.zeros((LANES - SAMPLE_T, GROUP_W), F32)
    kp = jnp.concatenate([k_new, pad], axis=0)
    vp = jnp.concatenate([v_new, pad], axis=0)
    s = _dot_nt(qblk[0:nq, :], kp) * scale
    key = lax.broadcasted_iota(jnp.int32, (nq, LANES), 1)
    qry = lax.broadcasted_iota(jnp.int32, (nq, LANES), 0) % SAMPLE_T
    s = jnp.where(key <= qry, s, NEG_BIG)
    m = s.max(axis=-1, keepdims=True)
    p = jnp.exp(s - m)
    return m, p.sum(axis=-1, keepdims=True), _dot(p, vp)


def _merge_cols(m_scr, l_scr, o_scr, sel, own, nblk, nq):
    m_own, l_own, o_own = own
    m_all = m_scr[0:nq, :]
    m_tot = jnp.maximum(jnp.max(jnp.where(sel, m_all, NEG_BIG), axis=-1, keepdims=True), m_own)
    w = jnp.where(sel, jnp.exp(m_all - m_tot), 0.0)
    e_own = jnp.exp(m_own - m_tot)
    den = jnp.sum(w * l_scr[0:nq, :], axis=-1, keepdims=True) + e_own * l_own
    num = e_own * o_own
    for n in range(nblk):
        num += w[:, n:n + 1] * o_scr[n]
    return num / den


def _sample_attn_kernel2(pt_ref, mq_ref, mk_ref, mv_ref, dq_ref, dk_ref, dv_ref, *rest, nblk, per_blk, bps, lam_init):
    per_step = per_blk * bps
    n_pages = 4 * per_step
    page_refs = rest[:n_pages]
    (mc_ref, msa_ref, msb_ref, dc_ref, dsa_ref, dsb_ref, dp_ref,
     ym_ref, yd_ref, mkout_ref, dkout_ref,
     qm_scr, qd_scr, g_scr, mm_scr, lm_scr, om_scr, md_scr, ld_scr, od_scr) = rest[n_pages:]
    n = pl.program_id(1)
    m_half = HEAD_DIM // 8
    d_half = DIFF_DH // 8
    m_scale = HEAD_DIM ** -0.5
    d_scale = DIFF_DH ** -0.5

    @pl.when(n == 0)
    def _():
        qm = _rope(mq_ref[...], mc_ref[...], msa_ref[...], msb_ref[...], m_half)
        qm_scr[...] = _block_diag_queries(qm, 1)
        qd = _rope(dq_ref[...], dc_ref[...], dsa_ref[...], dsb_ref[...], d_half)
        qd_scr[...] = _block_diag_queries(qd, 2)
        for scr in (g_scr, mm_scr, lm_scr, md_scr, ld_scr):
            scr[...] = jnp.zeros_like(scr)

    def pages(cache_idx, i):
        base = cache_idx * per_step + i * per_blk
        return [page_refs[base + j][...].reshape(GROUP_W, LANES) for j in range(per_blk)]

    lane_m = lax.broadcasted_iota(jnp.int32, (NQ_MOBA, LANES), 1)
    lane_d = lax.broadcasted_iota(jnp.int32, (NQ_DIFF, LANES), 1)
    qm_blk = qm_scr[...]
    qd_blk = qd_scr[...]
    mm, lm, gm = mm_scr[0:NQ_MOBA, :], lm_scr[0:NQ_MOBA, :], g_scr[0:NQ_MOBA, :]
    md, ld = md_scr[0:NQ_DIFF, :], ld_scr[0:NQ_DIFF, :]
    for i in range(bps):
        blk = n * bps + i
        kts, vts = pages(0, i), pages(1, i)
        m, l, o = _page_partials(kts, vts, qm_blk, m_scale, NQ_MOBA)
        mm = jnp.where(lane_m == blk, m, mm)
        lm = jnp.where(lane_m == blk, l, lm)
        om_scr[blk] = o
        k_mean = sum(kts).sum(axis=-1, keepdims=True) * (1.0 / MOBA_BLOCK)
        gate = _dot(qm_blk[0:NQ_MOBA, :], jnp.broadcast_to(k_mean, (GROUP_W, LANES)), HI)
        gm = jnp.where(lane_m == blk, gate, gm)
        m, l, o = _page_partials(pages(2, i), pages(3, i), qd_blk, d_scale, NQ_DIFF)
        md = jnp.where(lane_d == blk, m, md)
        ld = jnp.where(lane_d == blk, l, ld)
        od_scr[blk] = o
    mm_scr[0:NQ_MOBA, :], lm_scr[0:NQ_MOBA, :], g_scr[0:NQ_MOBA, :] = mm, lm, gm
    md_scr[0:NQ_DIFF, :], ld_scr[0:NQ_DIFF, :] = md, ld

    @pl.when(n == nblk // bps - 1)
    def _():
        k_new = _rope(mk_ref[...], mc_ref[...], msa_ref[...], msb_ref[...], m_half)
        mkout_ref[...] = k_new
        own = _own_partials(k_new, mv_ref[...], qm_scr[...], m_scale, NQ_MOBA)
        gate = jnp.where(lane_m < nblk, g_scr[0:NQ_MOBA, :], -jnp.inf)
        cnt = jnp.zeros((NQ_MOBA, LANES), F32)
        for j in range(nblk):
            gj = gate[:, j:j + 1]
            cnt += jnp.where((gj > gate) | ((gj == gate) & (j < lane_m)), 1.0, 0.0)
        sel = (lane_m < nblk) & (cnt < MOBA_TOPK)
        rows_m = _merge_cols(mm_scr, lm_scr, om_scr, sel, own, nblk, NQ_MOBA)
        ym_ref[...] = jnp.concatenate(
            [rows_m[h * SAMPLE_T:(h + 1) * SAMPLE_T, h * HEAD_DIM:(h + 1) * HEAD_DIM] for h in range(N_HEADS)], axis=-1)

        k_new = _rope(dk_ref[...], dc_ref[...], dsa_ref[...], dsb_ref[...], d_half)
        dkout_ref[...] = k_new
        own = _own_partials(k_new, dv_ref[...], qd_scr[...], d_scale, NQ_DIFF)
        rows_d = _merge_cols(md_scr, ld_scr, od_scr, lane_d < nblk, own, nblk, NQ_DIFF)
        dp = dp_ref[...]
        lam = _diff_lambda(dp, lam_init)
        subln = dp[4:5, 0:2 * DIFF_DH]
        outs = []
        for h in range(N_HEADS):
            o1 = rows_d[h * SAMPLE_T:(h + 1) * SAMPLE_T, h * HEAD_DIM:(h + 1) * HEAD_DIM]
            o2 = rows_d[NQ_MOBA + h * SAMPLE_T:NQ_MOBA + (h + 1) * SAMPLE_T, h * HEAD_DIM:(h + 1) * HEAD_DIM]
            outs.append(_rms(o1 - lam * o2, subln) * (1.0 - lam_init))
        yd_ref[...] = jnp.concatenate(outs, axis=-1)


SAMPLE_BLOCKS_PER_STEP = 4


def _sample_attn2(mq, mk, mv, dq, dk, dv, caches_t, page_table, tabs_m, tabs_d, dparams, lam_init, layer, n_seq):
    n_pages = page_table.shape[1]
    page = caches_t[0].shape[-1]
    per_blk = MOBA_BLOCK // page
    nblk = n_pages // per_blk
    assert page == LANES and nblk <= LANES
    bps = math.gcd(nblk, SAMPLE_BLOCKS_PER_STEP)
    per_step = per_blk * bps
    new = pl.BlockSpec((SAMPLE_T, GROUP_W), lambda b, n, pt: (b, 0))
    tab = pl.BlockSpec((SAMPLE_T, GROUP_W), lambda b, n, pt: (0, 0))

    def page_spec(j):
        return pl.BlockSpec((None, None, N_HEADS, HEAD_DIM, page),
                            lambda b, n, pt: (layer, pt[b, per_step * n + j], 0, 0, 0))

    cache_specs, cache_args = [], []
    for pool in caches_t:
        for j in range(per_step):
            cache_specs.append(page_spec(j))
            cache_args.append(pool)
    out_sd = jax.ShapeDtypeStruct((n_seq * SAMPLE_T, GROUP_W), F32)
    stat = pltpu.VMEM((LANES, LANES), F32)
    grid_spec = pltpu.PrefetchScalarGridSpec(
        num_scalar_prefetch=1,
        grid=(n_seq, nblk // bps),
        in_specs=[new] * 6 + cache_specs + [tab] * 6 + [pl.BlockSpec((SUBLANES, LANES), lambda b, n, pt: (0, 0))],
        out_specs=[new] * 4,
        scratch_shapes=[pltpu.VMEM((LANES, GROUP_W), F32), pltpu.VMEM((LANES, GROUP_W), F32),
                        stat, stat, stat, pltpu.VMEM((nblk, NQ_MOBA, GROUP_W), F32),
                        stat, stat, pltpu.VMEM((nblk, NQ_DIFF, GROUP_W), F32)],
    )
    return pl.pallas_call(
        functools.partial(_sample_attn_kernel2, nblk=nblk, per_blk=per_blk, bps=bps, lam_init=float(lam_init)),
        grid_spec=grid_spec,
        out_shape=[out_sd] * 4,
        compiler_params=_cparams(("arbitrary", "arbitrary"), 32),
        name="sample_attn",
    )(page_table, mq, mk, mv, dq, dk, dv, *cache_args, *tabs_m, *tabs_d, dparams)


def _reorder_w_in(w_in):
    o_a = GROUP_W + SSM_XBC + N_HEADS
    xbc = w_in[..., GROUP_W:GROUP_W + SSM_XBC]
    z = w_in[..., 0:GROUP_W]
    dt = jnp.pad(w_in[..., GROUP_W + SSM_XBC:o_a], ((0, 0), (0, 0), (0, LANES - N_HEADS)))
    rest = w_in[..., o_a:]
    return jnp.concatenate([xbc, z, rest, dt], axis=-1)


def kernel(x_prompt, x_sample, c_prompt, c_sample, cache_moba_k, cache_moba_v, cache_diff_k, cache_diff_v, page_table, state_ssm, state_conv, state_wkv, state_shift, w_ada, b_ada, norm_mix_pre, norm_mix_post, norm_ffn_pre, norm_ffn_post, w_in, w_out, conv_w, conv_b, dt_bias, a_log, d_skip, ssm_norm_w, lambda_q1, lambda_k1, lambda_q2, lambda_k2, subln_w, mu_shift, w0, w2, a0, a2, g2, k_k, k_a, r_k, ln_x_w, ln_x_b, w_gate, w_up, w_down):
    nbp, tp, d = x_prompt.shape
    nbs, ts, _ = x_sample.shape
    depth = w_in.shape[0]
    assert ts == SAMPLE_T and d == D_MODEL and tp % MOBA_BLOCK == 0
    past_len = page_table.shape[1] * cache_moba_k.shape[2]
    assert past_len % MOBA_BLOCK == 0

    w_in_b = _reorder_w_in(w_in).astype(BF16)
    w_out_b, w_gate_b, w_up_b, w_down_b = (w.astype(BF16) for w in (w_out, w_gate, w_up, w_down))
    mods = _ada_mod(jnp.concatenate([c_prompt, c_sample], axis=0), w_ada, b_ada)
    norm_pre = norm_mix_pre.reshape(depth, 1, d)
    norms3 = jnp.stack([norm_mix_post, norm_ffn_pre, norm_ffn_post], axis=1).reshape(depth, 3, 1, d)
    pos_p = jnp.arange(tp)
    pos_s = past_len + jnp.arange(ts)
    tabs_mp, tabs_dp = _rope_tables(pos_p, HEAD_DIM), _rope_tables(pos_p, DIFF_DH)
    tabs_ms, tabs_ds = _rope_tables(pos_s, HEAD_DIM), _rope_tables(pos_s, DIFF_DH)
    caches = [jnp.transpose(c, (0, 1, 3, 4, 2)) for c in (cache_moba_k, cache_moba_v, cache_diff_k, cache_diff_v)]

    xp = x_prompt.reshape(nbp * tp, d)
    xs = x_sample.reshape(nbs * ts, d)
    zeros_conv = jnp.zeros((nbp, SUBLANES, SSM_XBC), F32)
    zeros_ssm = jnp.zeros((nbp, N_HEADS, HEAD_DIM, SSM_STATE), F32)
    zeros_shift = jnp.zeros((nbp, 1, RWKV_PROJ), F32)
    zeros_wkv = jnp.zeros((nbp, N_HEADS, HEAD_DIM, HEAD_DIM), F32)
    new_p, new_s = [], []
    for l in range(depth):
        lam_init = 0.8 - 0.6 * math.exp(-0.3 * l)
        head_params = _ssd_head_params(dt_bias[l], a_log[l], d_skip[l])
        vec_params = _rwkv_vec_params(dict(mu_shift=mu_shift[l], w0=w0[l], a0=a0[l], k_k=k_k[l], k_a=k_a[l],
                                           r_k=r_k[l], ln_x_w=ln_x_w[l], ln_x_b=ln_x_b[l]))
        dparams = _diff_params(dict(lambda_q1=lambda_q1[l], lambda_k1=lambda_k1[l], lambda_q2=lambda_q2[l],
                                    lambda_k2=lambda_k2[l], subln_w=subln_w[l]))
        ssd_w = (conv_w[l], conv_b[l][None], head_params, ssm_norm_w[l][None])
        rwkv_w = (vec_params, w2[l], a2[l], g2[l])
        dense_w = (norms3, w_out_b, w_gate_b, w_up_b, w_down_b)

        mod_p = mods[l, :nbp].reshape(nbp, N_MOD, 1, d)
        pr = _in_proj(xp, mod_p, norm_pre, w_in_b, l, ROW_TILE)
        ya, conv_p, ssm_p = _ssd(pr['xbc'], pr['z'], pr['dt'], zeros_conv, zeros_ssm, *ssd_w, nbp, tp)
        yb, mk_p = _moba_prompt2(pr['mq'], pr['mk'], pr['mv'], tabs_mp, nbp, tp)
        yc, dk_p = _diff_prompt2(pr['dq'], pr['dk'], pr['dv'], tabs_dp, dparams, subln_w[l], lam_init, nbp, tp)
        yd, shift_p, wkv_p = _rwkv(pr['pd'], zeros_shift, zeros_wkv, *rwkv_w, nbp, tp)
        xp = _post(xp, (ya, yb, yc, yd), mod_p, *dense_w, l, ROW_TILE, FF_TILE)
        new_p.append((mk_p, pr['mv'], dk_p, pr['dv'], ssm_p, conv_p, wkv_p, shift_p))

        mod_s = jnp.repeat(mods[l, nbp:].reshape(nbs, N_MOD, d), ts, axis=0).transpose(1, 0, 2)[None]
        sr = _in_proj(xs, mod_s, norm_pre, w_in_b, l, nbs * ts)
        conv_prev8 = jnp.pad(state_conv[l], ((0, 0), (SUBLANES - (SSM_CONV - 1), 0), (0, 0)))
        ya, conv_s, ssm_s = _ssd(sr['xbc'], sr['z'], sr['dt'], conv_prev8, state_ssm[l], *ssd_w, nbs, ts)
        yb, yc, mk_s, dk_s = _sample_attn2(sr['mq'], sr['mk'], sr['mv'], sr['dq'], sr['dk'], sr['dv'], caches,
                                          page_table, tabs_ms, tabs_ds, dparams, lam_init, l, nbs)
        yd, shift_s, wkv_s = _rwkv(sr['pd'], state_shift[l][:, None, :], state_wkv[l], *rwkv_w, nbs, ts)
        xs = _post(xs, (ya, yb, yc, yd), mod_s, *dense_w, l, nbs * ts, FF_TILE)
        new_s.append((mk_s, sr['mv'], dk_s, sr['dv'], ssm_s, conv_s, wkv_s, shift_s))

    def collect(states, nb, t):
        kv_shape = (depth, nb, t, N_HEADS, HEAD_DIM)
        stacked = [jnp.stack([s[i] for s in states]) for i in range(8)]
        return tuple(a.reshape(kv_shape) for a in stacked[:4]) + tuple(stacked[4:])

    return ((xp.reshape(nbp, tp, d), xs.reshape(nbs, ts, d)) + collect(new_p, nbp, tp) + collect(new_s, nbs, ts))
```

```python
import functools
import math

import jax
import jax.numpy as jnp
from jax import lax
from jax.experimental import pallas as pl
from jax.experimental.pallas import tpu as pltpu

F32 = jnp.float32
BF16 = jnp.bfloat16
HI = lax.Precision.HIGHEST

V7X_VMEM_BYTES = 64 * 1024 * 1024
LANES = 128
SUBLANES = 8

D_MODEL = 1024
GROUP_W = 256
HEAD_DIM = 64
N_HEADS = 4
SSM_STATE = 128
SSM_CONV = 4
SSM_CHUNK = 128
SSM_XBC = 768
MOBA_BLOCK = 256
MOBA_TOPK = 3
DIFF_DH = 32
RWKV_PROJ = 1024
RWKV_CHUNK = 64
RWKV_LN_EPS = 64e-5
ROPE_THETA = 500000.0
D_FF = 2816
N_MOD = 6
RMS_EPS = 1e-6
NEG_BIG = -1e30

SEG_WIDTHS = (("xbc", 768), ("z", 256), ("mq", 256), ("mk", 256), ("mv", 256), ("dq", 256), ("dk", 256),
              ("dv", 256), ("pd", 1024), ("dt", 128))
IN_PROJ_PAD = sum(w for _, w in SEG_WIDTHS)
ROW_TILE = 512
FF_TILE = 1408


def _cparams(sem, vmem_mib):
    return pltpu.CompilerParams(dimension_semantics=sem, vmem_limit_bytes=vmem_mib * 1024 * 1024)


def _sigmoid(x):
    return 1.0 / (1.0 + jnp.exp(-x))


def _silu(x):
    return x * _sigmoid(x)


def _softplus(x):
    return jnp.maximum(x, 0.0) + jnp.log(1.0 + jnp.exp(-jnp.abs(x)))


def _mxu_operands(a, b, precision):
    if precision is None:
        return a.astype(BF16), b.astype(BF16)
    return a, b


def _dot(a, b, precision=None):
    a, b = _mxu_operands(a, b, precision)
    return jnp.dot(a, b, preferred_element_type=F32, precision=precision)


def _dot_nt(a, b, precision=None):
    a, b = _mxu_operands(a, b, precision)
    return lax.dot_general(a, b, (((1,), (1,)), ((), ())), preferred_element_type=F32, precision=precision)


def _dot_tn(a, b, precision=None):
    a, b = _mxu_operands(a, b, precision)
    return lax.dot_general(a, b, (((0,), (0,)), ((), ())), preferred_element_type=F32, precision=precision)


def _rms(x, w):
    return x * lax.rsqrt(jnp.mean(x * x, axis=-1, keepdims=True) + RMS_EPS) * w


def _mod_kernel(c_ref, w_ref, b_ref, o_ref):
    c = c_ref[...]
    o_ref[...] = _dot(_silu(c).astype(BF16), w_ref[...].astype(BF16)) + b_ref[...]


def _ada_mod(c_all, w_ada, b_ada):
    depth, _, n = w_ada.shape
    nb = c_all.shape[0]
    tn = 1536
    return pl.pallas_call(
        _mod_kernel,
        grid=(depth, n // tn),
        in_specs=[pl.BlockSpec((nb, D_MODEL), lambda l, j: (0, 0)),
                  pl.BlockSpec((None, D_MODEL, tn), lambda l, j: (l, 0, j)),
                  pl.BlockSpec((None, 1, tn), lambda l, j: (l, 0, j))],
        out_specs=pl.BlockSpec((None, nb, tn), lambda l, j: (l, 0, j)),
        out_shape=jax.ShapeDtypeStruct((depth, nb, n), F32),
        compiler_params=_cparams(("arbitrary", "arbitrary"), 40),
        name="ada_mod",
    )(c_all, w_ada, b_ada.reshape(depth, 1, n))


def _inproj_kernel(x_ref, mod_ref, nw_ref, w_ref, *out_refs):
    x = x_ref[...]
    h = _rms(x, nw_ref[...]) * (1.0 + mod_ref[1]) + mod_ref[0]
    hb = h.astype(BF16)
    off = 0
    for o_ref, (_, width) in zip(out_refs, SEG_WIDTHS):
        o_ref[...] = _dot(hb, w_ref[:, off:off + width])
        off += width


def _in_proj(x2d, mod4, norm_w, w_in_b, layer, tm):
    n_tok = x2d.shape[0]
    nb, _, r, _ = mod4.shape
    tiles_per_mod = (n_tok // tm) // nb
    out_shape = [jax.ShapeDtypeStruct((n_tok, w), F32) for _, w in SEG_WIDTHS]
    out_specs = [pl.BlockSpec((tm, w), lambda i: (i, 0)) for _, w in SEG_WIDTHS]
    outs = pl.pallas_call(
        _inproj_kernel,
        grid=(n_tok // tm,),
        in_specs=[pl.BlockSpec((tm, D_MODEL), lambda i: (i, 0)),
                  pl.BlockSpec((None, N_MOD, r, D_MODEL), lambda i: (i // tiles_per_mod, 0, 0, 0)),
                  pl.BlockSpec((None, 1, D_MODEL), lambda i: (layer, 0, 0)),
                  pl.BlockSpec((None, D_MODEL, IN_PROJ_PAD), lambda i: (layer, 0, 0))],
        out_specs=out_specs,
        out_shape=out_shape,
        compiler_params=_cparams(("arbitrary",), 56),
        name="in_proj",
    )(x2d, mod4, norm_w, w_in_b)
    return dict(zip([n for n, _ in SEG_WIDTHS], outs))


def _post_kernel(x_ref, ya_ref, yb_ref, yc_ref, yd_ref, mod_ref, nw_ref, wo_ref, wg_ref, wu_ref, wd_ref,
                 o_ref, x1_scr, h_scr, acc_scr):
    f = pl.program_id(1)

    @pl.when(f == 0)
    def _():
        mixed = _dot(ya_ref[...].astype(BF16), wo_ref[0:GROUP_W, :])
        mixed += _dot(yb_ref[...].astype(BF16), wo_ref[GROUP_W:2 * GROUP_W, :])
        mixed += _dot(yc_ref[...].astype(BF16), wo_ref[2 * GROUP_W:3 * GROUP_W, :])
        mixed += _dot(yd_ref[...].astype(BF16), wo_ref[3 * GROUP_W:4 * GROUP_W, :])
        x1 = x_ref[...] + (1.0 + mod_ref[2]) * _rms(mixed, nw_ref[0])
        x1_scr[...] = x1
        h = _rms(x1, nw_ref[1]) * (1.0 + mod_ref[4]) + mod_ref[3]
        h_scr[...] = h.astype(BF16)
        acc_scr[...] = jnp.zeros_like(acc_scr)

    hb = h_scr[...]
    a = _silu(_dot(hb, wg_ref[...])) * _dot(hb, wu_ref[...])
    acc_scr[...] += _dot(a.astype(BF16), wd_ref[...])

    @pl.when(f == pl.num_programs(1) - 1)
    def _():
        o_ref[...] = x1_scr[...] + (1.0 + mod_ref[5]) * _rms(acc_scr[...], nw_ref[2])


def _post(x2d, ys, mod4, norms3, w_out_b, w_gate_b, w_up_b, w_down_b, layer, tm, tf):
    n_tok = x2d.shape[0]
    nb, _, r, _ = mod4.shape
    tiles_per_mod = (n_tok // tm) // nb
    yspec = pl.BlockSpec((tm, GROUP_W), lambda i, f: (i, 0))
    return pl.pallas_call(
        _post_kernel,
        grid=(n_tok // tm, D_FF // tf),
        in_specs=[pl.BlockSpec((tm, D_MODEL), lambda i, f: (i, 0)), yspec, yspec, yspec, yspec,
                  pl.BlockSpec((None, N_MOD, r, D_MODEL), lambda i, f: (i // tiles_per_mod, 0, 0, 0)),
                  pl.BlockSpec((None, 3, 1, D_MODEL), lambda i, f: (layer, 0, 0, 0)),
                  pl.BlockSpec((None, D_MODEL, D_MODEL), lambda i, f: (layer, 0, 0)),
                  pl.BlockSpec((None, D_MODEL, tf), lambda i, f: (layer, 0, f)),
                  pl.BlockSpec((None, D_MODEL, tf), lambda i, f: (layer, 0, f)),
                  pl.BlockSpec((None, tf, D_MODEL), lambda i, f: (layer, f, 0))],
        out_specs=pl.BlockSpec((tm, D_MODEL), lambda i, f: (i, 0)),
        out_shape=jax.ShapeDtypeStruct((n_tok, D_MODEL), F32),
        scratch_shapes=[pltpu.VMEM((tm, D_MODEL), F32), pltpu.VMEM((tm, D_MODEL), BF16),
                        pltpu.VMEM((tm, D_MODEL), F32)],
        compiler_params=_cparams(("arbitrary", "arbitrary"), 56),
        name="post",
    )(x2d, ys[0], ys[1], ys[2], ys[3], mod4, norms3, w_out_b, w_gate_b, w_up_b, w_down_b)


def _ssd_kernel(xbc_ref, z_ref, dt_ref, cprev_ref, sprev_ref, cw_ref, cb_ref, hp_ref, nw_ref,
                y_ref, cnew_ref, snew_ref, buf, st, *, q):
    c = pl.program_id(1)

    @pl.when(c == 0)
    def _():
        buf[0:SUBLANES, :] = cprev_ref[...]
        st[...] = sprev_ref[...]

    xin = xbc_ref[...]
    buf[SUBLANES:SUBLANES + q, :] = xin
    conv = cb_ref[...] + cw_ref[3:4, :] * xin
    for i in range(SSM_CONV - 1):
        conv += cw_ref[i:i + 1, :] * buf[SUBLANES - 3 + i:SUBLANES - 3 + i + q, :]
    tail = buf[q:q + SUBLANES, :]
    buf[0:SUBLANES, :] = tail
    cnew_ref[...] = tail

    xc = _silu(conv)
    xs = xc[:, 0:GROUP_W]
    bm = xc[:, GROUP_W:GROUP_W + 2 * SSM_STATE]
    cm = xc[:, GROUP_W + 2 * SSM_STATE:]
    hp = hp_ref[...]
    dt = _softplus(dt_ref[...] + hp[0:1, :])
    da = dt * (-jnp.exp(hp[1:2, :]))
    rows = lax.broadcasted_iota(jnp.int32, (q, q), 0)
    cols = lax.broadcasted_iota(jnp.int32, (q, q), 1)
    tril = rows >= cols
    cs = _dot(tril.astype(F32), da, HI)
    onehot = (lax.broadcasted_iota(jnp.int32, (SUBLANES, LANES), 0)
              == lax.broadcasted_iota(jnp.int32, (SUBLANES, LANES), 1)).astype(F32)
    cs_t = _dot_nt(onehot, cs, HI)

    heads = range(N_HEADS)
    bgs = [bm[:, g * SSM_STATE:(g + 1) * SSM_STATE] for g in range(2)]
    cgs = [cm[:, g * SSM_STATE:(g + 1) * SSM_STATE] for g in range(2)]
    cbs = [_dot_nt(cgs[g], bgs[g]) for g in range(2)]
    cs_col = [cs[:, hd:hd + 1] for hd in heads]
    cs_last = [cs[q - 1:q, hd:hd + 1] for hd in heads]
    decay_in = [jnp.where(tril, jnp.exp(jnp.where(tril, cs_col[hd] - cs_t[hd:hd + 1, :], 0.0)), 0.0) for hd in heads]
    xh = [xs[:, hd * HEAD_DIM:(hd + 1) * HEAD_DIM] for hd in heads]
    xdt = [xh[hd] * dt[:, hd:hd + 1] for hd in heads]
    s_prev = [st[hd] for hd in heads]
    y_diag = [_dot(cbs[hd // 2] * decay_in[hd], xdt[hd]) for hd in heads]
    y_off = [_dot_nt(cgs[hd // 2], s_prev[hd]) for hd in heads]
    s_upd = [_dot_tn(xdt[hd] * jnp.exp(cs_last[hd] - cs_col[hd]), bgs[hd // 2]) for hd in heads]
    for hd in heads:
        st[hd] = jnp.exp(cs_last[hd]) * s_prev[hd] + s_upd[hd]
    y_heads = [y_diag[hd] + y_off[hd] * jnp.exp(cs_col[hd]) + hp[2:3, hd:hd + 1] * xh[hd] for hd in heads]
    y = jnp.concatenate(y_heads, axis=-1) * _silu(z_ref[...])
    y_ref[...] = _rms(y, nw_ref[...])

    @pl.when(c == pl.num_programs(1) - 1)
    def _():
        snew_ref[...] = st[...]


def _ssd_head_params(dt_bias, a_log, d_skip):
    rows = jnp.stack([dt_bias, a_log, d_skip]).astype(F32)
    return jnp.pad(rows, ((0, SUBLANES - 3), (0, LANES - N_HEADS)))


def _rwkv_vec_params(p):
    small = jnp.stack([p['w0'], p['a0'], p['k_k'], p['k_a'], p['r_k'].reshape(GROUP_W), p['ln_x_w'], p['ln_x_b']])
    return jnp.concatenate([p['mu_shift'][None, :], jnp.pad(small, ((0, 0), (0, RWKV_PROJ - GROUP_W)))], axis=0)


def _ssd(xbc, z, dt, conv_prev8, ssm_prev, conv_w, conv_b, head_params, norm_w, n_seq, seq_len):
    q = math.gcd(seq_len, SSM_CHUNK)
    nc = seq_len // q
    tok = lambda b, c: (b * nc + c, 0)
    fixed2 = lambda b, c: (0, 0)
    y, conv_new8, ssm_new = pl.pallas_call(
        functools.partial(_ssd_kernel, q=q),
        grid=(n_seq, nc),
        in_specs=[pl.BlockSpec((q, SSM_XBC), tok), pl.BlockSpec((q, GROUP_W), tok), pl.BlockSpec((q, LANES), tok),
                  pl.BlockSpec((None, SUBLANES, SSM_XBC), lambda b, c: (b, 0, 0)),
                  pl.BlockSpec((None, N_HEADS, HEAD_DIM, SSM_STATE), lambda b, c: (b, 0, 0, 0)),
                  pl.BlockSpec((SSM_CONV, SSM_XBC), fixed2), pl.BlockSpec((1, SSM_XBC), fixed2),
                  pl.BlockSpec((SUBLANES, LANES), fixed2), pl.BlockSpec((1, GROUP_W), fixed2)],
        out_specs=[pl.BlockSpec((q, GROUP_W), tok),
                   pl.BlockSpec((None, SUBLANES, SSM_XBC), lambda b, c: (b, 0, 0)),
                   pl.BlockSpec((None, N_HEADS, HEAD_DIM, SSM_STATE), lambda b, c: (b, 0, 0, 0))],
        out_shape=[jax.ShapeDtypeStruct((n_seq * seq_len, GROUP_W), F32),
                   jax.ShapeDtypeStruct((n_seq, SUBLANES, SSM_XBC), F32),
                   jax.ShapeDtypeStruct((n_seq, N_HEADS, HEAD_DIM, SSM_STATE), F32)],
        scratch_shapes=[pltpu.VMEM((q + SUBLANES, SSM_XBC), F32), pltpu.VMEM((N_HEADS, HEAD_DIM, SSM_STATE), F32)],
        compiler_params=_cparams(("arbitrary", "arbitrary"), 32),
        name="ssd",
    )(xbc, z, dt, conv_prev8, ssm_prev, conv_w, conv_b, head_params, norm_w)
    return y, conv_new8[:, SUBLANES - (SSM_CONV - 1):, :], ssm_new


def _cumsum_rows(x, seg):
    idx = lax.broadcasted_iota(jnp.int32, x.shape, 0) % seg
    s = 1
    while s < seg:
        x = x + jnp.where(idx >= s, pltpu.roll(x, s, 0), 0.0)
        s *= 2
    return x


def _rwkv_kernel(u_ref, shprev_ref, sprev_ref, vp_ref, w2_ref, a2_ref, g2_ref,
                  y_ref, shnew_ref, snew_ref, buf, st, *, c, g_chunks):
    ci = pl.program_id(1)
    n = c * g_chunks

    @pl.when(ci == 0)
    def _():
        buf[0:SUBLANES, :] = jnp.broadcast_to(shprev_ref[...], (SUBLANES, RWKV_PROJ))
        st[...] = sprev_ref[...]

    u = u_ref[...]
    buf[SUBLANES:SUBLANES + n, :] = u
    u_prev = buf[SUBLANES - 1:SUBLANES - 1 + n, :]
    buf[0:SUBLANES, :] = buf[n:n + SUBLANES, :]
    shnew_ref[...] = u[n - 1:n, :]

    vp = vp_ref[...]
    us = u + vp[0:1, :] * (u_prev - u)
    gw = GROUP_W
    r = us[:, 0:gw]
    k = us[:, gw:2 * gw]
    v = us[:, 2 * gw:3 * gw]
    wd = us[:, 3 * gw:3 * gw + 64]
    ad = us[:, 3 * gw + 64:3 * gw + 128]
    gd = us[:, 3 * gw + 128:]
    w0, a0, k_k, k_a = vp[1:2, 0:gw], vp[2:3, 0:gw], vp[3:4, 0:gw], vp[4:5, 0:gw]
    r_k, ln_w, ln_b = vp[5:6, 0:gw], vp[6:7, 0:gw], vp[7:8, 0:gw]
    w = -_softplus(-(w0 + _dot(jnp.tanh(wd), w2_ref[...]))) - 0.5
    a = _sigmoid(a0 + _dot(ad, a2_ref[...]))
    gate = _dot(_sigmoid(gd), g2_ref[...])
    kk_raw = k * k_k
    k2 = k * (1.0 + (a - 1.0) * k_a)
    lw = -jnp.exp(w)
    cum = _cumsum_rows(lw, c)
    e_in = jnp.exp(cum)
    e_out = jnp.exp(-cum)
    e_prev = jnp.exp(cum - lw)

    rows = lax.broadcasted_iota(jnp.int32, (c, c), 0)
    cols = lax.broadcasted_iota(jnp.int32, (c, c), 1)
    tril = rows >= cols
    stril = rows > cols
    eye = (rows == cols).astype(F32)
    units =[(gi, h) for gi in range(g_chunks) for h in range(N_HEADS)]

    def part(x, gi, h):
        return x[gi * c:(gi + 1) * c, h * HEAD_DIM:(h + 1) * HEAD_DIM]

    kk_heads = []
    for h in range(N_HEADS):
        kkh = kk_raw[:, h * HEAD_DIM:(h + 1) * HEAD_DIM]
        kk_heads.append(kkh / jnp.maximum(jnp.sqrt(jnp.sum(kkh * kkh, axis=-1, keepdims=True)), 1e-12))
    kk = jnp.concatenate(kk_heads, axis=-1)
    r_t_all = r * e_in
    k_t_all = k2 * e_out
    b_t_all = kk * a * e_out
    kap_t_all = kk * e_prev

    r_t = {q: part(r_t_all, *q) for q in units}
    k_t = {q: part(k_t_all, *q) for q in units}
    b_t = {q: part(b_t_all, *q) for q in units}
    kap_t = {q: part(kap_t_all, *q) for q in units}
    vv = {q: part(v, *q) for q in units}
    aa = {q: _dot_nt(jnp.concatenate([kap_t[q], r_t[q]], axis=0), jnp.concatenate([k_t[q], b_t[q]], axis=0))
          for q in units}
    a_kb = {q: jnp.where(stril, aa[q][0:c, c:2 * c], 0.0) for q in units}
    a_v = {q: jnp.concatenate([jnp.where(stril, aa[q][0:c, 0:c], 0.0), jnp.where(tril, aa[q][c:2 * c, 0:c], 0.0)],
                              axis=0) for q in units}
    a_rb = {q: jnp.where(tril, aa[q][c:2 * c, c:2 * c], 0.0) for q in units}
    base = min(RWKV_INV_BASE, c)

    def same_block(size):
        return (rows // size) == (cols // size)

    n_pow = {q: jnp.where(same_block(base), a_kb[q], 0.0) for q in units}
    t_inv = {q: eye - n_pow[q] for q in units}
    for _ in range(int(math.log2(base)) - 1):
        n_pow = {q: _dot(n_pow[q], n_pow[q]) for q in units}
        t_inv = {q: t_inv[q] + _dot(t_inv[q], n_pow[q]) for q in units}
    size = base
    while size < c:
        off = same_block(2 * size) & jnp.logical_not(same_block(size))
        ct = {q: _dot(jnp.where(off, a_kb[q], 0.0), t_inv[q]) for q in units}
        t_inv = {q: t_inv[q] - _dot(t_inv[q], ct[q]) for q in units}
        size *= 2
    av = {q: _dot(a_v[q], vv[q]) for q in units}
    wu = {q: _dot(t_inv[q], jnp.concatenate([kap_t[q], av[q][0:c]], axis=-1)) for q in units}
    arw = {q: _dot(a_rb[q], wu[q]) for q in units}
    r_hat = {q: r_t[q] - arw[q][:, 0:HEAD_DIM] for q in units}
    y_loc = {q: av[q][c:2 * c] - arw[q][:, HEAD_DIM:] for q in units}
    g_mat = {q: _dot_tn(wu[q][:, 0:HEAD_DIM], b_t[q]) for q in units}
    h_loc = {q: _dot_tn(jnp.concatenate([vv[q], wu[q][:, HEAD_DIM:]], axis=0),
                        jnp.concatenate([k_t[q], -b_t[q]], axis=0)) for q in units}

    y_heads = []
    for h in range(N_HEADS):
        s_cur = st[h]
        ys = []
        for gi in range(g_chunks):
            q = (gi, h)
            ys.append(_dot_nt(r_hat[q], s_cur) + y_loc[q])
            decay_end = e_in[(gi + 1) * c - 1:(gi + 1) * c, h * HEAD_DIM:(h + 1) * HEAD_DIM]
            s_cur = (s_cur - _dot(s_cur, g_mat[q]) + h_loc[q]) * decay_end
        st[h] = s_cur
        yh = jnp.concatenate(ys, axis=0) if g_chunks > 1 else ys[0]
        mu = jnp.mean(yh, axis=-1, keepdims=True)
        var = jnp.mean((yh - mu) ** 2, axis=-1, keepdims=True)
        y_heads.append((yh - mu) * lax.rsqrt(var + RWKV_LN_EPS))
    yn = jnp.concatenate(y_heads, axis=-1)
    rk = r * k2 * r_k
    bonus = jnp.concatenate(
        [jnp.sum(rk[:, h * HEAD_DIM:(h + 1) * HEAD_DIM], axis=-1, keepdims=True) * v[:, h * HEAD_DIM:(h + 1) * HEAD_DIM]
         for h in range(N_HEADS)], axis=-1)
    y_ref[...] = (yn * ln_w + ln_b + bonus) * gate

    @pl.when(ci == pl.num_programs(1) - 1)
    def _():
        snew_ref[...] = st[...]


RWKV_CHUNKS_PER_STEP = 8
RWKV_INV_BASE = 16


def _rwkv(pd, shift_prev, wkv_prev, vec_params, w2, a2, g2, n_seq, seq_len):
    c = math.gcd(seq_len, RWKV_CHUNK)
    g_chunks = math.gcd(seq_len // c, RWKV_CHUNKS_PER_STEP)
    n = c * g_chunks
    nc = seq_len // n
    tok = lambda b, i: (b * nc + i, 0)
    fixed2 = lambda b, i: (0, 0)
    y, shift_new, wkv_new = pl.pallas_call(
        functools.partial(_rwkv_kernel, c=c, g_chunks=g_chunks),
        grid=(n_seq, nc),
        in_specs=[pl.BlockSpec((n, RWKV_PROJ), tok),
                  pl.BlockSpec((None, 1, RWKV_PROJ), lambda b, i: (b, 0, 0)),
                  pl.BlockSpec((None, N_HEADS, HEAD_DIM, HEAD_DIM), lambda b, i: (b, 0, 0, 0)),
                  pl.BlockSpec((SUBLANES, RWKV_PROJ), fixed2),
                  pl.BlockSpec((64, GROUP_W), fixed2), pl.BlockSpec((64, GROUP_W), fixed2),
                  pl.BlockSpec((128, GROUP_W), fixed2)],
        out_specs=[pl.BlockSpec((n, GROUP_W), tok),
                   pl.BlockSpec((None, 1, RWKV_PROJ), lambda b, i: (b, 0, 0)),
                   pl.BlockSpec((None, N_HEADS, HEAD_DIM, HEAD_DIM), lambda b, i: (b, 0, 0, 0))],
        out_shape=[jax.ShapeDtypeStruct((n_seq * seq_len, GROUP_W), F32),
                   jax.ShapeDtypeStruct((n_seq, 1, RWKV_PROJ), F32),
                   jax.ShapeDtypeStruct((n_seq, N_HEADS, HEAD_DIM, HEAD_DIM), F32)],
        scratch_shapes=[pltpu.VMEM((n + SUBLANES, RWKV_PROJ), F32), pltpu.VMEM((N_HEADS, HEAD_DIM, HEAD_DIM), F32)],
        compiler_params=_cparams(("arbitrary", "arbitrary"), 32),
        name="rwkv",
    )(pd, shift_prev, wkv_prev, vec_params, w2, a2, g2)
    return y, shift_new[:, 0, :], wkv_new


def _rope_tables(pos, unit):
    rot = unit // 4
    half = rot // 2
    inv_freq = ROPE_THETA ** (-jnp.arange(half, dtype=jnp.float32) / half)
    ang = pos.astype(jnp.float32)[:, None] * inv_freq[None, :]
    cos, sin = jnp.cos(ang), jnp.sin(ang)
    t = pos.shape[0]
    zero_h = jnp.zeros((t, half), F32)
    rest0 = jnp.zeros((t, unit - rot), F32)
    c = jnp.concatenate([cos, cos, jnp.ones((t, unit - rot), F32)], axis=-1)
    sa = jnp.concatenate([-sin, zero_h, rest0], axis=-1)
    sb = jnp.concatenate([zero_h, sin, rest0], axis=-1)
    reps = GROUP_W // unit
    return tuple(jnp.tile(a, (1, reps)) for a in (c, sa, sb))


def _rope(x, c, sa, sb, half):
    n = x.shape[-1]
    return x * c + pltpu.roll(x, n - half, 1) * sa + pltpu.roll(x, half, 1) * sb


def _diff_params(p):
    rows = [jnp.pad(p[n], (0, LANES - DIFF_DH)) for n in ('lambda_q1', 'lambda_k1', 'lambda_q2', 'lambda_k2')]
    rows.append(jnp.pad(p['subln_w'], (0, LANES - 2 * DIFF_DH)))
    return jnp.pad(jnp.stack(rows), ((0, SUBLANES - 5), (0, 0)))


def _diff_lambda(dp, lam_init):
    return (jnp.exp(jnp.sum(dp[0:1, :] * dp[1:2, :], axis=-1, keepdims=True))
            - jnp.exp(jnp.sum(dp[2:3, :] * dp[3:4, :], axis=-1, keepdims=True)) + lam_init)


ATT_TILE = 256
ATT_KEYS = 256
ATT_SUB = ATT_TILE // ATT_KEYS
VT_ROWS = HEAD_DIM + SUBLANES


def _prep_kv(k_ref, v_ref, tabs, half, kt_out_ref, vt_out_ref, k_scr, vt_scr):
    c_ref, sa_ref, sb_ref = tabs
    kr = _rope(k_ref[...], c_ref[...], sa_ref[...], sb_ref[...], half)
    kt_out_ref[...] = kr.T
    vt32 = v_ref[...].T
    vt_out_ref[...] = vt32
    vt = vt32.astype(BF16)
    ones = jnp.ones((VT_ROWS - HEAD_DIM, vt.shape[1]), BF16)
    for h in range(N_HEADS):
        k_scr[h] = kr[:, h * HEAD_DIM:(h + 1) * HEAD_DIM].astype(BF16)
        vt_scr[h] = jnp.concatenate([vt[h * HEAD_DIM:(h + 1) * HEAD_DIM, :], ones], axis=0)
    return kr


def _kv_tiles(k_scr, vt_scr, h, off):
    return k_scr[h, pl.ds(off, ATT_KEYS), :], vt_scr[h, :, pl.ds(off, ATT_KEYS)]


def _attend_step(k_scr, vt_scr, chains, off, carries, keeps):
    tiles = {h: _kv_tiles(k_scr, vt_scr, h, off) for h in {h for h, _ in chains}}
    scores = [_dot(tiles[h][0], qs) for h, qs in chains]
    stats = []
    for s, carry, keep in zip(scores, carries, keeps):
        if keep is not None:
            s = jnp.where(keep, s, NEG_BIG)
        m_new = jnp.max(s, axis=0, keepdims=True)
        alpha = None
        if carry is not None:
            m_new = jnp.maximum(carry[0], m_new)
            alpha = jnp.exp(carry[0] - m_new)
        stats.append((m_new, alpha, jnp.exp((s - m_new).astype(BF16))))
    pvs = [_dot(tiles[h][1], st[2]) for (h, _), st in zip(chains, stats)]
    return tuple((st[0], pv if st[1] is None else st[1] * carry[1] + pv)
                 for st, pv, carry in zip(stats, pvs, carries))


def _attend_result(carry):
    acc = carry[1]
    return acc[0:HEAD_DIM, :] / acc[HEAD_DIM:HEAD_DIM + 1, :]


def _own_block(k_scr, vt_scr, chains, start):
    key_i = lax.broadcasted_iota(jnp.int32, (ATT_KEYS, ATT_TILE), 0)
    qry_i = lax.broadcasted_iota(jnp.int32, (ATT_KEYS, ATT_TILE), 1)
    carries = [None] * len(chains)
    for r in range(ATT_SUB):
        keep = key_i + r * ATT_KEYS <= qry_i
        carries = _attend_step(k_scr, vt_scr, chains, start + r * ATT_KEYS, carries, [keep] * len(chains))
    return carries


def _moba_prompt_kernel(q_ref, k_ref, v_ref, c_ref, sa_ref, sb_ref, kstack_ref, vstack_ref,
                        y_ref, kt_out_ref, vt_out_ref, k_scr, vt_scr, km_scr, *, nb):
    del kstack_ref, vstack_ref
    i = pl.program_id(1)
    tq = ATT_TILE
    half = HEAD_DIM // 8
    nbp = km_scr.shape[0]

    @pl.when(i == 0)
    def _():
        kr = _prep_kv(k_ref, v_ref, (c_ref, sa_ref, sb_ref), half, kt_out_ref, vt_out_ref, k_scr, vt_scr)
        km_scr[...] = jnp.zeros_like(km_scr)
        for n in range(nb):
            km_scr[n:n + 1, :] = jnp.mean(kr[n * tq:(n + 1) * tq, :], axis=0, keepdims=True)

    start = pl.multiple_of(i * tq, tq)
    q = _rope(q_ref[...], c_ref[pl.ds(start, tq), :], sa_ref[pl.ds(start, tq), :], sb_ref[pl.ds(start, tq), :], half)
    qt = q.T
    scale = HEAD_DIM ** -0.5
    blk_i = lax.broadcasted_iota(jnp.int32, (nbp, tq), 0)
    valid = blk_i < i

    chains, sels = [], []
    for h in range(N_HEADS):
        qth = qt[h * HEAD_DIM:(h + 1) * HEAD_DIM, :]
        gate = jnp.where(valid, _dot(km_scr[:, h * HEAD_DIM:(h + 1) * HEAD_DIM], qth, HI), -jnp.inf)
        cnt = jnp.zeros((nbp, tq), F32)
        for m in range(nb):
            gm = gate[m:m + 1, :]
            cnt += jnp.where((gm > gate) | ((gm == gate) & (m < blk_i)), 1.0, 0.0)
        sels.append(jnp.where(valid & (cnt < MOBA_TOPK), 1.0, 0.0))
        chains.append((h, (qth * scale).astype(BF16)))

    def body(n, carries):
        off = pl.multiple_of(n * ATT_KEYS, ATT_KEYS)
        chosen = [jnp.sum(jnp.where(blk_i == n // ATT_SUB, sel, 0.0), axis=0, keepdims=True) > 0.5 for sel in sels]
        return _attend_step(k_scr, vt_scr, chains, off, carries, chosen)

    fin = lax.fori_loop(0, i * ATT_SUB, body, tuple(_own_block(k_scr, vt_scr, chains, start)))
    yt = jnp.concatenate([_attend_result(carry) for carry in fin], axis=0)
    y_ref[...] = yt.T


def _kv_from_stack(stack):
    depth, n_seq, _, t = stack.shape
    return jnp.transpose(stack.reshape(depth, n_seq, N_HEADS, HEAD_DIM, t), (0, 1, 4, 2, 3))


def _kv_stack_specs(layer, seq_len):
    in_spec = pl.BlockSpec(memory_space=pl.ANY)
    out_spec = pl.BlockSpec((None, None, GROUP_W, seq_len), lambda b, i: (layer, b, 0, 0))
    return in_spec, out_spec


def _moba_prompt(q, k, v, tabs, k_stack, v_stack, layer, n_seq, seq_len):
    nb = seq_len // ATT_TILE
    nbp = -(-nb // SUBLANES) * SUBLANES
    seq = lambda b, i: (b, 0)
    fixed = lambda b, i: (0, 0)
    full = pl.BlockSpec((seq_len, GROUP_W), seq)
    tab = pl.BlockSpec((seq_len, GROUP_W), fixed)
    tile = pl.BlockSpec((ATT_TILE, GROUP_W), lambda b, i: (b * nb + i, 0))
    stack_in, stack_out = _kv_stack_specs(layer, seq_len)
    stack_sd = jax.ShapeDtypeStruct(k_stack.shape, F32)
    return pl.pallas_call(
        functools.partial(_moba_prompt_kernel, nb=nb),
        grid=(n_seq, nb),
        in_specs=[tile, full, full, tab, tab, tab, stack_in, stack_in],
        out_specs=[tile, stack_out, stack_out],
        out_shape=[jax.ShapeDtypeStruct((n_seq * seq_len, GROUP_W), F32), stack_sd, stack_sd],
        input_output_aliases={6: 1, 7: 2},
        scratch_shapes=[pltpu.VMEM((N_HEADS, seq_len, HEAD_DIM), BF16), pltpu.VMEM((N_HEADS, VT_ROWS, seq_len), BF16),
                        pltpu.VMEM((nbp, GROUP_W), F32)],
        compiler_params=_cparams(("arbitrary", "arbitrary"), 56),
        name="moba_prompt",
    )(q, k, v, *tabs, k_stack, v_stack)


def _diff_prompt_kernel(q_ref, k_ref, v_ref, c_ref, sa_ref, sb_ref, dp_ref, sw_ref, kstack_ref, vstack_ref,
                        y_ref, kt_out_ref, vt_out_ref, k_scr, vt_scr, *, lam_init):
    del kstack_ref, vstack_ref
    i = pl.program_id(1)
    tq = ATT_TILE
    half = DIFF_DH // 8

    @pl.when(i == 0)
    def _():
        _prep_kv(k_ref, v_ref, (c_ref, sa_ref, sb_ref), half, kt_out_ref, vt_out_ref, k_scr, vt_scr)

    start = pl.multiple_of(i * tq, tq)
    q = _rope(q_ref[...], c_ref[pl.ds(start, tq), :], sa_ref[pl.ds(start, tq), :], sb_ref[pl.ds(start, tq), :], half)
    qt = q.T
    scale = DIFF_DH ** -0.5
    dim_i = lax.broadcasted_iota(jnp.int32, (HEAD_DIM, tq), 0)

    chains = []
    for h in range(N_HEADS):
        qth = qt[h * HEAD_DIM:(h + 1) * HEAD_DIM, :] * scale
        for j in range(2):
            chains.append((h, jnp.where((dim_i >= DIFF_DH) == (j == 1), qth, 0.0).astype(BF16)))

    def body(n, carries):
        off = pl.multiple_of(n * ATT_KEYS, ATT_KEYS)
        return _attend_step(k_scr, vt_scr, chains, off, carries, [None] * len(chains))

    fin = lax.fori_loop(0, i * ATT_SUB, body, tuple(_own_block(k_scr, vt_scr, chains, start)))
    lam = _diff_lambda(dp_ref[...], lam_init)
    outs = []
    for h in range(N_HEADS):
        o = _attend_result(fin[2 * h]) - lam * _attend_result(fin[2 * h + 1])
        o = o * lax.rsqrt(jnp.mean(o * o, axis=0, keepdims=True) + RMS_EPS) * sw_ref[...]
        outs.append(o * (1.0 - lam_init))
    y_ref[...] = jnp.concatenate(outs, axis=0).T


def _diff_prompt(q, k, v, tabs, dparams, subln_w, lam_init, k_stack, v_stack, layer, n_seq, seq_len):
    nt = seq_len // ATT_TILE
    seq = lambda b, i: (b, 0)
    fixed = lambda b, i: (0, 0)
    full = pl.BlockSpec((seq_len, GROUP_W), seq)
    tab = pl.BlockSpec((seq_len, GROUP_W), fixed)
    tile = pl.BlockSpec((ATT_TILE, GROUP_W), lambda b, i: (b * nt + i, 0))
    subln_cols = jnp.broadcast_to(subln_w[:, None], (HEAD_DIM, ATT_TILE))
    stack_in, stack_out = _kv_stack_specs(layer, seq_len)
    stack_sd = jax.ShapeDtypeStruct(k_stack.shape, F32)
    return pl.pallas_call(
        functools.partial(_diff_prompt_kernel, lam_init=float(lam_init)),
        grid=(n_seq, nt),
        in_specs=[tile, full, full, tab, tab, tab, pl.BlockSpec((SUBLANES, LANES), fixed),
                  pl.BlockSpec((HEAD_DIM, ATT_TILE), fixed), stack_in, stack_in],
        out_specs=[tile, stack_out, stack_out],
        out_shape=[jax.ShapeDtypeStruct((n_seq * seq_len, GROUP_W), F32), stack_sd, stack_sd],
        input_output_aliases={8: 1, 9: 2},
        scratch_shapes=[pltpu.VMEM((N_HEADS, seq_len, HEAD_DIM), BF16), pltpu.VMEM((N_HEADS, VT_ROWS, seq_len), BF16)],
        compiler_params=_cparams(("arbitrary", "arbitrary"), 56),
        name="diff_prompt",
    )(q, k, v, *tabs, dparams, subln_cols, k_stack, v_stack)


SAMPLE_T = 8
NQ_MOBA = N_HEADS * SAMPLE_T
NQ_DIFF = 2 * N_HEADS * SAMPLE_T


def _block_diag_queries(q, n_maps):
    nq = n_maps * N_HEADS * SAMPLE_T
    tiled = jnp.concatenate([q] * (nq // SAMPLE_T) + [jnp.zeros((LANES - nq, GROUP_W), F32)], axis=0)
    row = lax.broadcasted_iota(jnp.int32, (LANES, GROUP_W), 0)
    lane = lax.broadcasted_iota(jnp.int32, (LANES, GROUP_W), 1)
    width = HEAD_DIM // n_maps
    keep = (row < nq) & ((row // SAMPLE_T) % N_HEADS == lane // HEAD_DIM) & (row // (N_HEADS * SAMPLE_T) == (lane % HEAD_DIM) // width)
    return jnp.where(keep, tiled, 0.0)


def _own_partials(k_new, v_new, qblk, scale, nq):
    pad = jnp.zeros((LANES - SAMPLE_T, GROUP_W), F32)
    kp = jnp.concatenate([k_new, pad], axis=0)
    vp = jnp.concatenate([v_new, pad], axis=0)
    s = _dot_nt(qblk[0:nq, :], kp) * scale
    key = lax.broadcasted_iota(jnp.int32, (nq, LANES), 1)
    qry = lax.broadcasted_iota(jnp.int32, (nq, LANES), 0) % SAMPLE_T
    s = jnp.where(key <= qry, s, NEG_BIG)
    m = s.max(axis=-1, keepdims=True)
    p = jnp.exp(s - m)
    return m, p.sum(axis=-1, keepdims=True), _dot(p, vp)


def _merge_cols(m_scr, l_scr, o_scr, sel, own, nblk, nq):
    m_own, l_own, o_own = own
    m_all = m_scr[0:nq, :]
    m_tot = jnp.maximum(jnp.max(jnp.where(sel, m_all, NEG_BIG), axis=-1, keepdims=True), m_own)
    w = jnp.where(sel, jnp.exp(m_all - m_tot), 0.0)
    e_own = jnp.exp(m_own - m_tot)
    den = jnp.sum(w * l_scr[0:nq, :], axis=-1, keepdims=True) + e_own * l_own
    num = e_own * o_own
    for n in range(nblk):
        num += w[:, n:n + 1] * o_scr[n]
    return num / den


def _sample_attn_kernel(pt_ref, mq_ref, mk_ref, mv_ref, dq_ref, dk_ref, dv_ref, *rest, nblk, per_blk, bps, lam_init):
    per_step = per_blk * bps
    n_pages = 4 * per_step
    page_refs = rest[:n_pages]
    (mc_ref, msa_ref, msb_ref, dc_ref, dsa_ref, dsb_ref, dp_ref,
     ym_ref, yd_ref, mkout_ref, dkout_ref,
     qm_scr, qd_scr, g_scr, mm_scr, lm_scr, om_scr, md_scr, ld_scr, od_scr) = rest[n_pages:]
    n = pl.program_id(1)
    m_half = HEAD_DIM // 8
    d_half = DIFF_DH // 8
    m_scale = HEAD_DIM ** -0.5
    d_scale = DIFF_DH ** -0.5

    @pl.when(n == 0)
    def _():
        qm = _rope(mq_ref[...], mc_ref[...], msa_ref[...], msb_ref[...], m_half)
        qm_scr[...] = _block_diag_queries(qm, 1)
        qd = _rope(dq_ref[...], dc_ref[...], dsa_ref[...], dsb_ref[...], d_half)
        qd_scr[...] = _block_diag_queries(qd, 2)
        for scr in (g_scr, mm_scr, lm_scr, md_scr, ld_scr):
            scr[...] = jnp.zeros_like(scr)

    def pages(cache_idx, i):
        base = cache_idx * per_step + i * per_blk
        return [page_refs[base + j][...].reshape(GROUP_W, LANES) for j in range(per_blk)]

    lane_m = lax.broadcasted_iota(jnp.int32, (NQ_MOBA, LANES), 1)
    lane_d = lax.broadcasted_iota(jnp.int32, (NQ_DIFF, LANES), 1)
    qm_blk = qm_scr[...]
    qd_blk = qd_scr[...]
    mm, lm, gm = mm_scr[0:NQ_MOBA, :], lm_scr[0:NQ_MOBA, :], g_scr[0:NQ_MOBA, :]
    md, ld = md_scr[0:NQ_DIFF, :], ld_scr[0:NQ_DIFF, :]
    units = [(i, kind) for i in range(bps) for kind in (0, 1)]
    qbs = (qm_blk[0:NQ_MOBA, :].astype(BF16), qd_blk[0:NQ_DIFF, :].astype(BF16))
    scales = (m_scale, d_scale)
    kts = {(i, kind): pages(2 * kind, i) for i, kind in units}
    vts = {(i, kind): pages(2 * kind + 1, i) for i, kind in units}
    scores = {u: [_dot(qbs[u[1]], kt) * scales[u[1]] for kt in kts[u]] for u in units}
    k_means = [sum(kts[(i, 0)]).sum(axis=-1, keepdims=True) * (1.0 / MOBA_BLOCK) for i in range(bps)]
    gates = [_dot(qm_blk[0:NQ_MOBA, :], jnp.broadcast_to(km, (GROUP_W, LANES)), HI) for km in k_means]
    probs, stats = {}, {}
    for u in units:
        m = scores[u][0].max(axis=-1, keepdims=True)
        for s in scores[u][1:]:
            m = jnp.maximum(m, s.max(axis=-1, keepdims=True))
        probs[u] = [jnp.exp(s - m) for s in scores[u]]
        stats[u] = (m, sum(p.sum(axis=-1, keepdims=True) for p in probs[u]))
    outs = {u: sum(_dot_nt(p, vt) for p, vt in zip(probs[u], vts[u])) for u in units}
    for i in range(bps):
        blk = n * bps + i
        mm = jnp.where(lane_m == blk, stats[(i, 0)][0], mm)
        lm = jnp.where(lane_m == blk, stats[(i, 0)][1], lm)
        gm = jnp.where(lane_m == blk, gates[i], gm)
        md = jnp.where(lane_d == blk, stats[(i, 1)][0], md)
        ld = jnp.where(lane_d == blk, stats[(i, 1)][1], ld)
        om_scr[blk] = outs[(i, 0)]
        od_scr[blk] = outs[(i, 1)]
    mm_scr[0:NQ_MOBA, :], lm_scr[0:NQ_MOBA, :], g_scr[0:NQ_MOBA, :] = mm, lm, gm
    md_scr[0:NQ_DIFF, :], ld_scr[0:NQ_DIFF, :] = md, ld

    @pl.when(n == nblk // bps - 1)
    def _():
        k_new = _rope(mk_ref[...], mc_ref[...], msa_ref[...], msb_ref[...], m_half)
        mkout_ref[...] = k_new
        own = _own_partials(k_new, mv_ref[...], qm_scr[...], m_scale, NQ_MOBA)
        gate = jnp.where(lane_m < nblk, g_scr[0:NQ_MOBA, :], -jnp.inf)
        cnt = jnp.zeros((NQ_MOBA, LANES), F32)
        for j in range(nblk):
            gj = gate[:, j:j + 1]
            cnt += jnp.where((gj > gate) | ((gj == gate) & (j < lane_m)), 1.0, 0.0)
        sel = (lane_m < nblk) & (cnt < MOBA_TOPK)
        rows_m = _merge_cols(mm_scr, lm_scr, om_scr, sel, own, nblk, NQ_MOBA)
        ym_ref[...] = jnp.concatenate(
            [rows_m[h * SAMPLE_T:(h + 1) * SAMPLE_T, h * HEAD_DIM:(h + 1) * HEAD_DIM] for h in range(N_HEADS)], axis=-1)

        k_new = _rope(dk_ref[...], dc_ref[...], dsa_ref[...], dsb_ref[...], d_half)
        dkout_ref[...] = k_new
        own = _own_partials(k_new, dv_ref[...], qd_scr[...], d_scale, NQ_DIFF)
        rows_d = _merge_cols(md_scr, ld_scr, od_scr, lane_d < nblk, own, nblk, NQ_DIFF)
        dp = dp_ref[...]
        lam = _diff_lambda(dp, lam_init)
        subln = dp[4:5, 0:2 * DIFF_DH]
        outs = []
        for h in range(N_HEADS):
            o1 = rows_d[h * SAMPLE_T:(h + 1) * SAMPLE_T, h * HEAD_DIM:(h + 1) * HEAD_DIM]
            o2 = rows_d[NQ_MOBA + h * SAMPLE_T:NQ_MOBA + (h + 1) * SAMPLE_T, h * HEAD_DIM:(h + 1) * HEAD_DIM]
            outs.append(_rms(o1 - lam * o2, subln) * (1.0 - lam_init))
        yd_ref[...] = jnp.concatenate(outs, axis=-1)


SAMPLE_BLOCKS_PER_STEP = 4


def _sample_attn(mq, mk, mv, dq, dk, dv, caches_t, page_table, tabs_m, tabs_d, dparams, lam_init, layer, n_seq):
    n_pages = page_table.shape[1]
    page = caches_t[0].shape[-1]
    per_blk = MOBA_BLOCK // page
    nblk = n_pages // per_blk
    assert page == LANES and nblk <= LANES
    bps = math.gcd(nblk, SAMPLE_BLOCKS_PER_STEP)
    per_step = per_blk * bps
    new = pl.BlockSpec((SAMPLE_T, GROUP_W), lambda b, n, pt: (b, 0))
    tab = pl.BlockSpec((SAMPLE_T, GROUP_W), lambda b, n, pt: (0, 0))

    def page_spec(j):
        return pl.BlockSpec((None, None, N_HEADS, HEAD_DIM, page),
                            lambda b, n, pt: (layer, pt[b, per_step * n + j], 0, 0, 0))

    cache_specs, cache_args = [], []
    for pool in caches_t:
        for j in range(per_step):
            cache_specs.append(page_spec(j))
            cache_args.append(pool)
    out_sd = jax.ShapeDtypeStruct((n_seq * SAMPLE_T, GROUP_W), F32)
    stat = pltpu.VMEM((LANES, LANES), F32)
    grid_spec = pltpu.PrefetchScalarGridSpec(
        num_scalar_prefetch=1,
        grid=(n_seq, nblk // bps),
        in_specs=[new] * 6 + cache_specs + [tab] * 6 + [pl.BlockSpec((SUBLANES, LANES), lambda b, n, pt: (0, 0))],
        out_specs=[new] * 4,
        scratch_shapes=[pltpu.VMEM((LANES, GROUP_W), F32), pltpu.VMEM((LANES, GROUP_W), F32),
                        stat, stat, stat, pltpu.VMEM((nblk, NQ_MOBA, GROUP_W), F32),
                        stat, stat, pltpu.VMEM((nblk, NQ_DIFF, GROUP_W), F32)],
    )
    return pl.pallas_call(
        functools.partial(_sample_attn_kernel, nblk=nblk, per_blk=per_blk, bps=bps, lam_init=float(lam_init)),
        grid_spec=grid_spec,
        out_shape=[out_sd] * 4,
        compiler_params=_cparams(("arbitrary", "arbitrary"), 32),
        name="sample_attn",
    )(page_table, mq, mk, mv, dq, dk, dv, *cache_args, *tabs_m, *tabs_d, dparams)


def _reorder_w_in(w_in):
    o_a = GROUP_W + SSM_XBC + N_HEADS
    xbc = w_in[..., GROUP_W:GROUP_W + SSM_XBC]
    z = w_in[..., 0:GROUP_W]
    dt = jnp.pad(w_in[..., GROUP_W + SSM_XBC:o_a], ((0, 0), (0, 0), (0, LANES - N_HEADS)))
    rest = w_in[..., o_a:]
    return jnp.concatenate([xbc, z, rest, dt], axis=-1)


def kernel(x_prompt, x_sample, c_prompt, c_sample, cache_moba_k, cache_moba_v, cache_diff_k, cache_diff_v, page_table, state_ssm, state_conv, state_wkv, state_shift, w_ada, b_ada, norm_mix_pre, norm_mix_post, norm_ffn_pre, norm_ffn_post, w_in, w_out, conv_w, conv_b, dt_bias, a_log, d_skip, ssm_norm_w, lambda_q1, lambda_k1, lambda_q2, lambda_k2, subln_w, mu_shift, w0, w2, a0, a2, g2, k_k, k_a, r_k, ln_x_w, ln_x_b, w_gate, w_up, w_down):
    nbp, tp, d = x_prompt.shape
    nbs, ts, _ = x_sample.shape
    depth = w_in.shape[0]
    assert ts == SAMPLE_T and d == D_MODEL and tp % MOBA_BLOCK == 0
    past_len = page_table.shape[1] * cache_moba_k.shape[2]
    assert past_len % MOBA_BLOCK == 0

    w_in_b = _reorder_w_in(w_in).astype(BF16)
    w_out_b, w_gate_b, w_up_b, w_down_b = (w.astype(BF16) for w in (w_out, w_gate, w_up, w_down))
    mods = _ada_mod(jnp.concatenate([c_prompt, c_sample], axis=0), w_ada, b_ada)
    norm_pre = norm_mix_pre.reshape(depth, 1, d)
    norms3 = jnp.stack([norm_mix_post, norm_ffn_pre, norm_ffn_post], axis=1).reshape(depth, 3, 1, d)
    pos_p = jnp.arange(tp)
    pos_s = past_len + jnp.arange(ts)
    tabs_mp, tabs_dp = _rope_tables(pos_p, HEAD_DIM), _rope_tables(pos_p, DIFF_DH)
    tabs_ms, tabs_ds = _rope_tables(pos_s, HEAD_DIM), _rope_tables(pos_s, DIFF_DH)
    caches = [jnp.transpose(c, (0, 1, 3, 4, 2)) for c in (cache_moba_k, cache_moba_v, cache_diff_k, cache_diff_v)]

    xp = x_prompt.reshape(nbp * tp, d)
    xs = x_sample.reshape(nbs * ts, d)
    zeros_conv = jnp.zeros((nbp, SUBLANES, SSM_XBC), F32)
    zeros_ssm = jnp.zeros((nbp, N_HEADS, HEAD_DIM, SSM_STATE), F32)
    zeros_shift = jnp.zeros((nbp, 1, RWKV_PROJ), F32)
    zeros_wkv = jnp.zeros((nbp, N_HEADS, HEAD_DIM, HEAD_DIM), F32)
    kv_p = [jnp.zeros((depth, nbp, GROUP_W, tp), F32) for _ in range(4)]
    new_p, new_s = [], []
    for l in range(depth):
        lam_init = 0.8 - 0.6 * math.exp(-0.3 * l)
        head_params = _ssd_head_params(dt_bias[l], a_log[l], d_skip[l])
        vec_params = _rwkv_vec_params(dict(mu_shift=mu_shift[l], w0=w0[l], a0=a0[l], k_k=k_k[l], k_a=k_a[l],
                                           r_k=r_k[l], ln_x_w=ln_x_w[l], ln_x_b=ln_x_b[l]))
        dparams = _diff_params(dict(lambda_q1=lambda_q1[l], lambda_k1=lambda_k1[l], lambda_q2=lambda_q2[l],
                                    lambda_k2=lambda_k2[l], subln_w=subln_w[l]))
        ssd_w = (conv_w[l], conv_b[l][None], head_params, ssm_norm_w[l][None])
        rwkv_w = (vec_params, w2[l], a2[l], g2[l])
        dense_w = (norms3, w_out_b, w_gate_b, w_up_b, w_down_b)

        mod_p = mods[l, :nbp].reshape(nbp, N_MOD, 1, d)
        pr = _in_proj(xp, mod_p, norm_pre, w_in_b, l, ROW_TILE)
        ya, conv_p, ssm_p = _ssd(pr['xbc'], pr['z'], pr['dt'], zeros_conv, zeros_ssm, *ssd_w, nbp, tp)
        yb, kv_p[0], kv_p[1] = _moba_prompt(pr['mq'], pr['mk'], pr['mv'], tabs_mp, kv_p[0], kv_p[1], l, nbp, tp)
        yc, kv_p[2], kv_p[3] = _diff_prompt(pr['dq'], pr['dk'], pr['dv'], tabs_dp, dparams, subln_w[l], lam_init,
                                            kv_p[2], kv_p[3], l, nbp, tp)
        yd, shift_p, wkv_p = _rwkv(pr['pd'], zeros_shift, zeros_wkv, *rwkv_w, nbp, tp)
        xp = _post(xp, (ya, yb, yc, yd), mod_p, *dense_w, l, ROW_TILE, FF_TILE)
        new_p.append((ssm_p, conv_p, wkv_p, shift_p))

        mod_s = jnp.repeat(mods[l, nbp:].reshape(nbs, N_MOD, d), ts, axis=0).transpose(1, 0, 2)[None]
        sr = _in_proj(xs, mod_s, norm_pre, w_in_b, l, nbs * ts)
        conv_prev8 = jnp.pad(state_conv[l], ((0, 0), (SUBLANES - (SSM_CONV - 1), 0), (0, 0)))
        ya, conv_s, ssm_s = _ssd(sr['xbc'], sr['z'], sr['dt'], conv_prev8, state_ssm[l], *ssd_w, nbs, ts)
        yb, yc, mk_s, dk_s = _sample_attn(sr['mq'], sr['mk'], sr['mv'], sr['dq'], sr['dk'], sr['dv'], caches,
                                          page_table, tabs_ms, tabs_ds, dparams, lam_init, l, nbs)
        yd, shift_s, wkv_s = _rwkv(sr['pd'], state_shift[l][:, None, :], state_wkv[l], *rwkv_w, nbs, ts)
        xs = _post(xs, (ya, yb, yc, yd), mod_s, *dense_w, l, nbs * ts, FF_TILE)
        new_s.append((mk_s, sr['mv'], dk_s, sr['dv'], ssm_s, conv_s, wkv_s, shift_s))

    def stack(states, i):
        return jnp.stack([s[i] for s in states])

    kv_s_shape = (depth, nbs, ts, N_HEADS, HEAD_DIM)
    out_p = tuple(_kv_from_stack(a) for a in kv_p) + tuple(stack(new_p, i) for i in range(4))
    out_s = tuple(stack(new_s, i).reshape(kv_s_shape) for i in range(4)) + tuple(stack(new_s, i) for i in range(4, 8))
    return (xp.reshape(nbp, tp, d), xs.reshape(nbs, ts, d)) + out_p + out_s
```

```python
import functools
import math

import jax
import jax.numpy as jnp
from jax import lax
from jax.experimental import pallas as pl
from jax.experimental.pallas import tpu as pltpu

F32 = jnp.float32
BF16 = jnp.bfloat16
HI = lax.Precision.HIGHEST

V7X_VMEM_BYTES = 64 * 1024 * 1024
LANES = 128
SUBLANES = 8

D_MODEL = 1024
GROUP_W = 256
HEAD_DIM = 64
N_HEADS = 4
SSM_STATE = 128
SSM_CONV = 4
SSM_CHUNK = 128
SSM_XBC = 768
MOBA_BLOCK = 256
MOBA_TOPK = 3
DIFF_DH = 32
RWKV_PROJ = 1024
RWKV_CHUNK = 64
RWKV_LN_EPS = 64e-5
ROPE_THETA = 500000.0
D_FF = 2816
N_MOD = 6
RMS_EPS = 1e-6
NEG_BIG = -1e30

SEG_WIDTHS = (("xbc", 768), ("z", 256), ("mq", 256), ("mk", 256), ("mv", 256), ("dq", 256), ("dk", 256),
              ("dv", 256), ("pd", 1024), ("dt", 128))
IN_PROJ_PAD = sum(w for _, w in SEG_WIDTHS)
ROW_TILE = 512
FF_TILE = 1408


def _cparams(sem, vmem_mib):
    return pltpu.CompilerParams(dimension_semantics=sem, vmem_limit_bytes=vmem_mib * 1024 * 1024)


def _sigmoid(x):
    return 1.0 / (1.0 + jnp.exp(-x))


def _silu(x):
    return x * _sigmoid(x)


def _softplus(x):
    return jnp.maximum(x, 0.0) + jnp.log(1.0 + jnp.exp(-jnp.abs(x)))


def _mxu_operands(a, b, precision):
    if precision is None:
        return a.astype(BF16), b.astype(BF16)
    return a, b


def _dot(a, b, precision=None):
    a, b = _mxu_operands(a, b, precision)
    return jnp.dot(a, b, preferred_element_type=F32, precision=precision)


def _dot_nt(a, b, precision=None):
    a, b = _mxu_operands(a, b, precision)
    return lax.dot_general(a, b, (((1,), (1,)), ((), ())), preferred_element_type=F32, precision=precision)


def _dot_tn(a, b, precision=None):
    a, b = _mxu_operands(a, b, precision)
    return lax.dot_general(a, b, (((0,), (0,)), ((), ())), preferred_element_type=F32, precision=precision)


def _rms(x, w):
    return x * lax.rsqrt(jnp.mean(x * x, axis=-1, keepdims=True) + RMS_EPS) * w


def _mod_kernel(c_ref, w_ref, b_ref, o_ref):
    c = c_ref[...]
    o_ref[...] = _dot(_silu(c).astype(BF16), w_ref[...].astype(BF16)) + b_ref[...]


def _ada_mod(c_all, w_ada, b_ada):
    depth, _, n = w_ada.shape
    nb = c_all.shape[0]
    tn = 1536
    return pl.pallas_call(
        _mod_kernel,
        grid=(depth, n // tn),
        in_specs=[pl.BlockSpec((nb, D_MODEL), lambda l, j: (0, 0)),
                  pl.BlockSpec((None, D_MODEL, tn), lambda l, j: (l, 0, j)),
                  pl.BlockSpec((None, 1, tn), lambda l, j: (l, 0, j))],
        out_specs=pl.BlockSpec((None, nb, tn), lambda l, j: (l, 0, j)),
        out_shape=jax.ShapeDtypeStruct((depth, nb, n), F32),
        compiler_params=_cparams(("arbitrary", "arbitrary"), 40),
        name="ada_mod",
    )(c_all, w_ada, b_ada.reshape(depth, 1, n))


def _inproj_kernel(x_ref, mod_ref, nw_ref, w_ref, *out_refs):
    x = x_ref[...]
    h = _rms(x, nw_ref[...]) * (1.0 + mod_ref[1]) + mod_ref[0]
    hb = h.astype(BF16)
    off = 0
    for o_ref, (_, width) in zip(out_refs, SEG_WIDTHS):
        o_ref[...] = _dot(hb, w_ref[:, off:off + width])
        off += width


def _in_proj(x2d, mod4, norm_w, w_in_b, layer, tm):
    n_tok = x2d.shape[0]
    nb, _, r, _ = mod4.shape
    tiles_per_mod = (n_tok // tm) // nb
    out_shape = [jax.ShapeDtypeStruct((n_tok, w), F32) for _, w in SEG_WIDTHS]
    out_specs = [pl.BlockSpec((tm, w), lambda i: (i, 0)) for _, w in SEG_WIDTHS]
    outs = pl.pallas_call(
        _inproj_kernel,
        grid=(n_tok // tm,),
        in_specs=[pl.BlockSpec((tm, D_MODEL), lambda i: (i, 0)),
                  pl.BlockSpec((None, N_MOD, r, D_MODEL), lambda i: (i // tiles_per_mod, 0, 0, 0)),
                  pl.BlockSpec((None, 1, D_MODEL), lambda i: (layer, 0, 0)),
                  pl.BlockSpec((None, D_MODEL, IN_PROJ_PAD), lambda i: (layer, 0, 0))],
        out_specs=out_specs,
        out_shape=out_shape,
        compiler_params=_cparams(("arbitrary",), 56),
        name="in_proj",
    )(x2d, mod4, norm_w, w_in_b)
    return dict(zip([n for n, _ in SEG_WIDTHS], outs))


def _post_kernel(x_ref, ya_ref, yb_ref, yc_ref, yd_ref, mod_ref, nw_ref, wo_ref, wg_ref, wu_ref, wd_ref,
                 o_ref, x1_scr, h_scr, acc_scr):
    f = pl.program_id(1)

    @pl.when(f == 0)
    def _():
        mixed = _dot(ya_ref[...].astype(BF16), wo_ref[0:GROUP_W, :])
        mixed += _dot(yb_ref[...].astype(BF16), wo_ref[GROUP_W:2 * GROUP_W, :])
        mixed += _dot(yc_ref[...].astype(BF16), wo_ref[2 * GROUP_W:3 * GROUP_W, :])
        mixed += _dot(yd_ref[...].astype(BF16), wo_ref[3 * GROUP_W:4 * GROUP_W, :])
        x1 = x_ref[...] + (1.0 + mod_ref[2]) * _rms(mixed, nw_ref[0])
        x1_scr[...] = x1
        h = _rms(x1, nw_ref[1]) * (1.0 + mod_ref[4]) + mod_ref[3]
        h_scr[...] = h.astype(BF16)
        acc_scr[...] = jnp.zeros_like(acc_scr)

    hb = h_scr[...]
    a = _silu(_dot(hb, wg_ref[...])) * _dot(hb, wu_ref[...])
    acc_scr[...] += _dot(a.astype(BF16), wd_ref[...])

    @pl.when(f == pl.num_programs(1) - 1)
    def _():
        o_ref[...] = x1_scr[...] + (1.0 + mod_ref[5]) * _rms(acc_scr[...], nw_ref[2])


def _post(x2d, ys, mod4, norms3, w_out_b, w_gate_b, w_up_b, w_down_b, layer, tm, tf):
    n_tok = x2d.shape[0]
    nb, _, r, _ = mod4.shape
    tiles_per_mod = (n_tok // tm) // nb
    yspec = pl.BlockSpec((tm, GROUP_W), lambda i, f: (i, 0))
    return pl.pallas_call(
        _post_kernel,
        grid=(n_tok // tm, D_FF // tf),
        in_specs=[pl.BlockSpec((tm, D_MODEL), lambda i, f: (i, 0)), yspec, yspec, yspec, yspec,
                  pl.BlockSpec((None, N_MOD, r, D_MODEL), lambda i, f: (i // tiles_per_mod, 0, 0, 0)),
                  pl.BlockSpec((None, 3, 1, D_MODEL), lambda i, f: (layer, 0, 0, 0)),
                  pl.BlockSpec((None, D_MODEL, D_MODEL), lambda i, f: (layer, 0, 0)),
                  pl.BlockSpec((None, D_MODEL, tf), lambda i, f: (layer, 0, f)),
                  pl.BlockSpec((None, D_MODEL, tf), lambda i, f: (layer, 0, f)),
                  pl.BlockSpec((None, tf, D_MODEL), lambda i, f: (layer, f, 0))],
        out_specs=pl.BlockSpec((tm, D_MODEL), lambda i, f: (i, 0)),
        out_shape=jax.ShapeDtypeStruct((n_tok, D_MODEL), F32),
        scratch_shapes=[pltpu.VMEM((tm, D_MODEL), F32), pltpu.VMEM((tm, D_MODEL), BF16),
                        pltpu.VMEM((tm, D_MODEL), F32)],
        compiler_params=_cparams(("arbitrary", "arbitrary"), 56),
        name="post",
    )(x2d, ys[0], ys[1], ys[2], ys[3], mod4, norms3, w_out_b, w_gate_b, w_up_b, w_down_b)


def _ssd_kernel(xbc_ref, z_ref, dt_ref, cprev_ref, sprev_ref, cw_ref, cb_ref, hp_ref, nw_ref,
                y_ref, cnew_ref, snew_ref, buf, st, *, q, g_chunks):
    c = pl.program_id(1)
    n = q * g_chunks

    @pl.when(c == 0)
    def _():
        buf[0:SUBLANES, :] = cprev_ref[...]
        st[...] = sprev_ref[...]

    xin = xbc_ref[...]
    buf[SUBLANES:SUBLANES + n, :] = xin
    conv = cb_ref[...] + cw_ref[3:4, :] * xin
    for i in range(SSM_CONV - 1):
        conv += cw_ref[i:i + 1, :] * buf[SUBLANES - 3 + i:SUBLANES - 3 + i + n, :]
    tail = buf[n:n + SUBLANES, :]
    buf[0:SUBLANES, :] = tail
    cnew_ref[...] = tail

    xc = _silu(conv)
    xs = xc[:, 0:GROUP_W]
    bm = xc[:, GROUP_W:GROUP_W + 2 * SSM_STATE]
    cm = xc[:, GROUP_W + 2 * SSM_STATE:]
    hp = hp_ref[...]
    dt = _softplus(dt_ref[...] + hp[0:1, :])
    cs = _cumsum_rows(dt * (-jnp.exp(hp[1:2, :])), q)
    rows = lax.broadcasted_iota(jnp.int32, (q, q), 0)
    cols = lax.broadcasted_iota(jnp.int32, (q, q), 1)
    tril = rows >= cols
    onehot = (lax.broadcasted_iota(jnp.int32, (SUBLANES, LANES), 0)
              == lax.broadcasted_iota(jnp.int32, (SUBLANES, LANES), 1)).astype(F32)

    chunks = range(g_chunks)
    units = [(gi, hd) for gi in chunks for hd in range(N_HEADS)]
    rsl = [slice(gi * q, (gi + 1) * q) for gi in chunks]
    cs_t = [_dot_nt(onehot, cs[rsl[gi], :], HI) for gi in chunks]
    bgs = {(gi, g): bm[rsl[gi], g * SSM_STATE:(g + 1) * SSM_STATE] for gi in chunks for g in range(2)}
    cgs = {(gi, g): cm[rsl[gi], g * SSM_STATE:(g + 1) * SSM_STATE] for gi in chunks for g in range(2)}
    cbs = {k: _dot_nt(cgs[k], bgs[k]) for k in bgs}
    cs_col = {(gi, hd): cs[rsl[gi], hd:hd + 1] for gi, hd in units}
    cs_last = {(gi, hd): cs[(gi + 1) * q - 1:(gi + 1) * q, hd:hd + 1] for gi, hd in units}
    decay_in = {u: jnp.where(tril, jnp.exp(jnp.where(tril, cs_col[u] - cs_t[u[0]][u[1]:u[1] + 1, :], 0.0)), 0.0)
                for u in units}
    xh = {(gi, hd): xs[rsl[gi], hd * HEAD_DIM:(hd + 1) * HEAD_DIM] for gi, hd in units}
    xdt = {(gi, hd): xh[(gi, hd)] * dt[rsl[gi], hd:hd + 1] for gi, hd in units}
    y_diag = {(gi, hd): _dot(cbs[(gi, hd // 2)] * decay_in[(gi, hd)], xdt[(gi, hd)]) for gi, hd in units}
    s_upd = {(gi, hd): _dot_tn(xdt[(gi, hd)] * jnp.exp(cs_last[(gi, hd)] - cs_col[(gi, hd)]), bgs[(gi, hd // 2)])
             for gi, hd in units}
    s_in = {}
    for hd in range(N_HEADS):
        s_cur = st[hd]
        for gi in chunks:
            s_in[(gi, hd)] = s_cur
            s_cur = jnp.exp(cs_last[(gi, hd)]) * s_cur + s_upd[(gi, hd)]
        st[hd] = s_cur
    y_off = {(gi, hd): _dot_nt(cgs[(gi, hd // 2)], s_in[(gi, hd)]) for gi, hd in units}
    y_rows = []
    for gi in chunks:
        y_rows.append(jnp.concatenate(
            [y_diag[(gi, hd)] + y_off[(gi, hd)] * jnp.exp(cs_col[(gi, hd)]) + hp[2:3, hd:hd + 1] * xh[(gi, hd)]
             for hd in range(N_HEADS)], axis=-1))
    y = (jnp.concatenate(y_rows, axis=0) if g_chunks > 1 else y_rows[0]) * _silu(z_ref[...])
    y_ref[...] = _rms(y, nw_ref[...])

    @pl.when(c == pl.num_programs(1) - 1)
    def _():
        snew_ref[...] = st[...]


def _ssd_head_params(dt_bias, a_log, d_skip):
    rows = jnp.stack([dt_bias, a_log, d_skip]).astype(F32)
    return jnp.pad(rows, ((0, SUBLANES - 3), (0, LANES - N_HEADS)))


def _rwkv_vec_params(p):
    small = jnp.stack([p['w0'], p['a0'], p['k_k'], p['k_a'], p['r_k'].reshape(GROUP_W), p['ln_x_w'], p['ln_x_b']])
    return jnp.concatenate([p['mu_shift'][None, :], jnp.pad(small, ((0, 0), (0, RWKV_PROJ - GROUP_W)))], axis=0)


SSD_CHUNKS_PER_STEP = 4


def _ssd(xbc, z, dt, conv_prev8, ssm_prev, conv_w, conv_b, head_params, norm_w, n_seq, seq_len):
    q = math.gcd(seq_len, SSM_CHUNK)
    g_chunks = math.gcd(seq_len // q, SSD_CHUNKS_PER_STEP)
    n = q * g_chunks
    nc = seq_len // n
    tok = lambda b, c: (b * nc + c, 0)
    fixed2 = lambda b, c: (0, 0)
    y, conv_new8, ssm_new = pl.pallas_call(
        functools.partial(_ssd_kernel, q=q, g_chunks=g_chunks),
        grid=(n_seq, nc),
        in_specs=[pl.BlockSpec((n, SSM_XBC), tok), pl.BlockSpec((n, GROUP_W), tok), pl.BlockSpec((n, LANES), tok),
                  pl.BlockSpec((None, SUBLANES, SSM_XBC), lambda b, c: (b, 0, 0)),
                  pl.BlockSpec((None, N_HEADS, HEAD_DIM, SSM_STATE), lambda b, c: (b, 0, 0, 0)),
                  pl.BlockSpec((SSM_CONV, SSM_XBC), fixed2), pl.BlockSpec((1, SSM_XBC), fixed2),
                  pl.BlockSpec((SUBLANES, LANES), fixed2), pl.BlockSpec((1, GROUP_W), fixed2)],
        out_specs=[pl.BlockSpec((n, GROUP_W), tok),
                   pl.BlockSpec((None, SUBLANES, SSM_XBC), lambda b, c: (b, 0, 0)),
                   pl.BlockSpec((None, N_HEADS, HEAD_DIM, SSM_STATE), lambda b, c: (b, 0, 0, 0))],
        out_shape=[jax.ShapeDtypeStruct((n_seq * seq_len, GROUP_W), F32),
                   jax.ShapeDtypeStruct((n_seq, SUBLANES, SSM_XBC), F32),
                   jax.ShapeDtypeStruct((n_seq, N_HEADS, HEAD_DIM, SSM_STATE), F32)],
        scratch_shapes=[pltpu.VMEM((n + SUBLANES, SSM_XBC), F32), pltpu.VMEM((N_HEADS, HEAD_DIM, SSM_STATE), F32)],
        compiler_params=_cparams(("arbitrary", "arbitrary"), 32),
        name="ssd",
    )(xbc, z, dt, conv_prev8, ssm_prev, conv_w, conv_b, head_params, norm_w)
    return y, conv_new8[:, SUBLANES - (SSM_CONV - 1):, :], ssm_new


def _cumsum_rows(x, seg):
    idx = lax.broadcasted_iota(jnp.int32, x.shape, 0) % seg
    s = 1
    while s < seg:
        x = x + jnp.where(idx >= s, pltpu.roll(x, s, 0), 0.0)
        s *= 2
    return x


def _rwkv_kernel(u_ref, shprev_ref, sprev_ref, vp_ref, w2_ref, a2_ref, g2_ref,
                  y_ref, shnew_ref, snew_ref, buf, st, *, c, g_chunks):
    ci = pl.program_id(1)
    n = c * g_chunks

    @pl.when(ci == 0)
    def _():
        buf[0:SUBLANES, :] = jnp.broadcast_to(shprev_ref[...], (SUBLANES, RWKV_PROJ))
        st[...] = sprev_ref[...]

    u = u_ref[...]
    buf[SUBLANES:SUBLANES + n, :] = u
    u_prev = buf[SUBLANES - 1:SUBLANES - 1 + n, :]
    buf[0:SUBLANES, :] = buf[n:n + SUBLANES, :]
    shnew_ref[...] = u[n - 1:n, :]

    vp = vp_ref[...]
    us = u + vp[0:1, :] * (u_prev - u)
    gw = GROUP_W
    r = us[:, 0:gw]
    k = us[:, gw:2 * gw]
    v = us[:, 2 * gw:3 * gw]
    wd = us[:, 3 * gw:3 * gw + 64]
    ad = us[:, 3 * gw + 64:3 * gw + 128]
    gd = us[:, 3 * gw + 128:]
    w0, a0, k_k, k_a = vp[1:2, 0:gw], vp[2:3, 0:gw], vp[3:4, 0:gw], vp[4:5, 0:gw]
    r_k, ln_w, ln_b = vp[5:6, 0:gw], vp[6:7, 0:gw], vp[7:8, 0:gw]
    w = -_softplus(-(w0 + _dot(jnp.tanh(wd), w2_ref[...]))) - 0.5
    a = _sigmoid(a0 + _dot(ad, a2_ref[...]))
    gate = _dot(_sigmoid(gd), g2_ref[...])
    kk_raw = k * k_k
    k2 = k * (1.0 + (a - 1.0) * k_a)
    lw = -jnp.exp(w)
    cum = _cumsum_rows(lw, c)
    e_in = jnp.exp(cum)
    e_out = jnp.exp(-cum)
    e_prev = jnp.exp(cum - lw)

    rows = lax.broadcasted_iota(jnp.int32, (c, c), 0)
    cols = lax.broadcasted_iota(jnp.int32, (c, c), 1)
    tril = rows >= cols
    stril = rows > cols
    eye = (rows == cols).astype(F32)
    units =[(gi, h) for gi in range(g_chunks) for h in range(N_HEADS)]

    def part(x, gi, h):
        return x[gi * c:(gi + 1) * c, h * HEAD_DIM:(h + 1) * HEAD_DIM]

    head_ones = (lax.broadcasted_iota(jnp.int32, (gw, gw), 0) // HEAD_DIM
                 == lax.broadcasted_iota(jnp.int32, (gw, gw), 1) // HEAD_DIM).astype(BF16)

    def head_sums(x):
        return _dot(x, head_ones)

    kk = kk_raw * lax.rsqrt(jnp.maximum(head_sums(kk_raw * kk_raw), 1e-24))
    r_t_all = r * e_in
    k_t_all = k2 * e_out
    b_t_all = kk * a * e_out
    kap_t_all = kk * e_prev

    r_t = {q: part(r_t_all, *q) for q in units}
    k_t = {q: part(k_t_all, *q) for q in units}
    b_t = {q: part(b_t_all, *q) for q in units}
    kap_t = {q: part(kap_t_all, *q) for q in units}
    vv = {q: part(v, *q) for q in units}
    aa = {q: _dot_nt(jnp.concatenate([kap_t[q], r_t[q]], axis=0), jnp.concatenate([k_t[q], b_t[q]], axis=0))
          for q in units}
    a_kb = {q: jnp.where(stril, aa[q][0:c, c:2 * c], 0.0) for q in units}
    a_v = {q: jnp.concatenate([jnp.where(stril, aa[q][0:c, 0:c], 0.0), jnp.where(tril, aa[q][c:2 * c, 0:c], 0.0)],
                              axis=0) for q in units}
    a_rb = {q: jnp.where(tril, aa[q][c:2 * c, c:2 * c], 0.0) for q in units}
    base = min(RWKV_INV_BASE, c)

    def same_block(size):
        return (rows // size) == (cols // size)

    n_pow = {q: jnp.where(same_block(base), a_kb[q], 0.0) for q in units}
    t_inv = {q: eye - n_pow[q] for q in units}
    for _ in range(int(math.log2(base)) - 1):
        n_pow = {q: _dot(n_pow[q], n_pow[q]) for q in units}
        t_inv = {q: t_inv[q] + _dot(t_inv[q], n_pow[q]) for q in units}
    size = base
    while size < c:
        off = same_block(2 * size) & jnp.logical_not(same_block(size))
        ct = {q: _dot(jnp.where(off, a_kb[q], 0.0), t_inv[q]) for q in units}
        t_inv = {q: t_inv[q] - _dot(t_inv[q], ct[q]) for q in units}
        size *= 2
    av = {q: _dot(a_v[q], vv[q]) for q in units}
    wu = {q: _dot(t_inv[q], jnp.concatenate([kap_t[q], av[q][0:c]], axis=-1)) for q in units}
    arw = {q: _dot(a_rb[q], wu[q]) for q in units}
    r_hat = {q: r_t[q] - arw[q][:, 0:HEAD_DIM] for q in units}
    y_loc = {q: av[q][c:2 * c] - arw[q][:, HEAD_DIM:] for q in units}
    g_mat = {q: _dot_tn(wu[q][:, 0:HEAD_DIM], b_t[q]) for q in units}
    h_loc = {q: _dot_tn(jnp.concatenate([vv[q], wu[q][:, HEAD_DIM:]], axis=0),
                        jnp.concatenate([k_t[q], -b_t[q]], axis=0)) for q in units}

    y_heads = []
    for h in range(N_HEADS):
        s_cur = st[h]
        ys = []
        for gi in range(g_chunks):
            q = (gi, h)
            ys.append(_dot_nt(r_hat[q], s_cur) + y_loc[q])
            decay_end = e_in[(gi + 1) * c - 1:(gi + 1) * c, h * HEAD_DIM:(h + 1) * HEAD_DIM]
            s_cur = (s_cur - _dot(s_cur, g_mat[q]) + h_loc[q]) * decay_end
        st[h] = s_cur
        y_heads.append(jnp.concatenate(ys, axis=0) if g_chunks > 1 else ys[0])
    yy = jnp.concatenate(y_heads, axis=-1)
    inv_dim = 1.0 / HEAD_DIM
    yc = yy - head_sums(yy) * inv_dim
    yn = yc * lax.rsqrt(head_sums(yc * yc) * inv_dim + RWKV_LN_EPS)
    bonus = head_sums(r * k2 * r_k) * v
    y_ref[...] = (yn * ln_w + ln_b + bonus) * gate

    @pl.when(ci == pl.num_programs(1) - 1)
    def _():
        snew_ref[...] = st[...]


RWKV_CHUNKS_PER_STEP = 8
RWKV_INV_BASE = 16


def _rwkv(pd, shift_prev, wkv_prev, vec_params, w2, a2, g2, n_seq, seq_len):
    c = math.gcd(seq_len, RWKV_CHUNK)
    g_chunks = math.gcd(seq_len // c, RWKV_CHUNKS_PER_STEP)
    n = c * g_chunks
    nc = seq_len // n
    tok = lambda b, i: (b * nc + i, 0)
    fixed2 = lambda b, i: (0, 0)
    y, shift_new, wkv_new = pl.pallas_call(
        functools.partial(_rwkv_kernel, c=c, g_chunks=g_chunks),
        grid=(n_seq, nc),
        in_specs=[pl.BlockSpec((n, RWKV_PROJ), tok),
                  pl.BlockSpec((None, 1, RWKV_PROJ), lambda b, i: (b, 0, 0)),
                  pl.BlockSpec((None, N_HEADS, HEAD_DIM, HEAD_DIM), lambda b, i: (b, 0, 0, 0)),
                  pl.BlockSpec((SUBLANES, RWKV_PROJ), fixed2),
                  pl.BlockSpec((64, GROUP_W), fixed2), pl.BlockSpec((64, GROUP_W), fixed2),
                  pl.BlockSpec((128, GROUP_W), fixed2)],
        out_specs=[pl.BlockSpec((n, GROUP_W), tok),
                   pl.BlockSpec((None, 1, RWKV_PROJ), lambda b, i: (b, 0, 0)),
                   pl.BlockSpec((None, N_HEADS, HEAD_DIM, HEAD_DIM), lambda b, i: (b, 0, 0, 0))],
        out_shape=[jax.ShapeDtypeStruct((n_seq * seq_len, GROUP_W), F32),
                   jax.ShapeDtypeStruct((n_seq, 1, RWKV_PROJ), F32),
                   jax.ShapeDtypeStruct((n_seq, N_HEADS, HEAD_DIM, HEAD_DIM), F32)],
        scratch_shapes=[pltpu.VMEM((n + SUBLANES, RWKV_PROJ), F32), pltpu.VMEM((N_HEADS, HEAD_DIM, HEAD_DIM), F32)],
        compiler_params=_cparams(("arbitrary", "arbitrary"), 32),
        name="rwkv",
    )(pd, shift_prev, wkv_prev, vec_params, w2, a2, g2)
    return y, shift_new[:, 0, :], wkv_new


def _rope_tables(pos, unit):
    rot = unit // 4
    half = rot // 2
    inv_freq = ROPE_THETA ** (-jnp.arange(half, dtype=jnp.float32) / half)
    ang = pos.astype(jnp.float32)[:, None] * inv_freq[None, :]
    cos, sin = jnp.cos(ang), jnp.sin(ang)
    t = pos.shape[0]
    zero_h = jnp.zeros((t, half), F32)
    rest0 = jnp.zeros((t, unit - rot), F32)
    c = jnp.concatenate([cos, cos, jnp.ones((t, unit - rot), F32)], axis=-1)
    sa = jnp.concatenate([-sin, zero_h, rest0], axis=-1)
    sb = jnp.concatenate([zero_h, sin, rest0], axis=-1)
    reps = GROUP_W // unit
    return tuple(jnp.tile(a, (1, reps)) for a in (c, sa, sb))


def _rope(x, c, sa, sb, half):
    n = x.shape[-1]
    return x * c + pltpu.roll(x, n - half, 1) * sa + pltpu.roll(x, half, 1) * sb


def _diff_params(p):
    rows = [jnp.pad(p[n], (0, LANES - DIFF_DH)) for n in ('lambda_q1', 'lambda_k1', 'lambda_q2', 'lambda_k2')]
    rows.append(jnp.pad(p['subln_w'], (0, LANES - 2 * DIFF_DH)))
    return jnp.pad(jnp.stack(rows), ((0, SUBLANES - 5), (0, 0)))


def _diff_lambda(dp, lam_init):
    return (jnp.exp(jnp.sum(dp[0:1, :] * dp[1:2, :], axis=-1, keepdims=True))
            - jnp.exp(jnp.sum(dp[2:3, :] * dp[3:4, :], axis=-1, keepdims=True)) + lam_init)


ATT_TILE = 256
ATT_KEYS = 256
ATT_SUB = ATT_TILE // ATT_KEYS
VT_ROWS = HEAD_DIM + SUBLANES


def _prep_kv(k_ref, v_ref, tabs, half, kt_out_ref, vt_out_ref, k_scr, vt_scr):
    c_ref, sa_ref, sb_ref = tabs
    kr = _rope(k_ref[...], c_ref[...], sa_ref[...], sb_ref[...], half)
    kt_out_ref[...] = kr.T
    vt32 = v_ref[...].T
    vt_out_ref[...] = vt32
    vt = vt32.astype(BF16)
    ones = jnp.ones((VT_ROWS - HEAD_DIM, vt.shape[1]), BF16)
    for h in range(N_HEADS):
        k_scr[h] = kr[:, h * HEAD_DIM:(h + 1) * HEAD_DIM].astype(BF16)
        vt_scr[h] = jnp.concatenate([vt[h * HEAD_DIM:(h + 1) * HEAD_DIM, :], ones], axis=0)
    return kr


def _kv_tiles(k_scr, vt_scr, h, off):
    return k_scr[h, pl.ds(off, ATT_KEYS), :], vt_scr[h, :, pl.ds(off, ATT_KEYS)]


def _attend_step(k_scr, vt_scr, chains, off, carries, keeps):
    tiles = {h: _kv_tiles(k_scr, vt_scr, h, off) for h in {h for h, _ in chains}}
    scores = [_dot(tiles[h][0], qs) for h, qs in chains]
    stats = []
    for s, carry, keep in zip(scores, carries, keeps):
        if keep is not None:
            s = jnp.where(keep, s, NEG_BIG)
        m_new = jnp.max(s, axis=0, keepdims=True)
        alpha = None
        if carry is not None:
            m_new = jnp.maximum(carry[0], m_new)
            alpha = jnp.exp(carry[0] - m_new)
        stats.append((m_new, alpha, jnp.exp((s - m_new).astype(BF16))))
    pvs = [_dot(tiles[h][1], st[2]) for (h, _), st in zip(chains, stats)]
    return tuple((st[0], pv if st[1] is None else st[1] * carry[1] + pv)
                 for st, pv, carry in zip(stats, pvs, carries))


def _attend_result(carry):
    acc = carry[1]
    return acc[0:HEAD_DIM, :] / acc[HEAD_DIM:HEAD_DIM + 1, :]


def _own_block(k_scr, vt_scr, chains, start):
    key_i = lax.broadcasted_iota(jnp.int32, (ATT_KEYS, ATT_TILE), 0)
    qry_i = lax.broadcasted_iota(jnp.int32, (ATT_KEYS, ATT_TILE), 1)
    carries = [None] * len(chains)
    for r in range(ATT_SUB):
        keep = key_i + r * ATT_KEYS <= qry_i
        carries = _attend_step(k_scr, vt_scr, chains, start + r * ATT_KEYS, carries, [keep] * len(chains))
    return carries


def _moba_prompt_kernel(q_ref, k_ref, v_ref, c_ref, sa_ref, sb_ref, kstack_ref, vstack_ref,
                        y_ref, kt_out_ref, vt_out_ref, k_scr, vt_scr, km_scr, *, nb):
    del kstack_ref, vstack_ref
    i = pl.program_id(1)
    tq = ATT_TILE
    half = HEAD_DIM // 8
    nbp = km_scr.shape[0]

    @pl.when(i == 0)
    def _():
        kr = _prep_kv(k_ref, v_ref, (c_ref, sa_ref, sb_ref), half, kt_out_ref, vt_out_ref, k_scr, vt_scr)
        km_scr[...] = jnp.zeros_like(km_scr)
        for n in range(nb):
            km_scr[n:n + 1, :] = jnp.mean(kr[n * tq:(n + 1) * tq, :], axis=0, keepdims=True)

    start = pl.multiple_of(i * tq, tq)
    q = _rope(q_ref[...], c_ref[pl.ds(start, tq), :], sa_ref[pl.ds(start, tq), :], sb_ref[pl.ds(start, tq), :], half)
    qt = q.T
    scale = HEAD_DIM ** -0.5
    blk_i = lax.broadcasted_iota(jnp.int32, (nbp, tq), 0)
    valid = blk_i < i

    chains, sels = [], []
    for h in range(N_HEADS):
        qth = qt[h * HEAD_DIM:(h + 1) * HEAD_DIM, :]
        gate = jnp.where(valid, _dot(km_scr[:, h * HEAD_DIM:(h + 1) * HEAD_DIM], qth, HI), -jnp.inf)
        cnt = jnp.zeros((nbp, tq), F32)
        for m in range(nb):
            gm = gate[m:m + 1, :]
            cnt += jnp.where((gm > gate) | ((gm == gate) & (m < blk_i)), 1.0, 0.0)
        sels.append(jnp.where(valid & (cnt < MOBA_TOPK), 1.0, 0.0))
        chains.append((h, (qth * scale).astype(BF16)))

    def body(n, carries):
        off = pl.multiple_of(n * ATT_KEYS, ATT_KEYS)
        chosen = [jnp.sum(jnp.where(blk_i == n // ATT_SUB, sel, 0.0), axis=0, keepdims=True) > 0.5 for sel in sels]
        return _attend_step(k_scr, vt_scr, chains, off, carries, chosen)

    fin = lax.fori_loop(0, i * ATT_SUB, body, tuple(_own_block(k_scr, vt_scr, chains, start)))
    yt = jnp.concatenate([_attend_result(carry) for carry in fin], axis=0)
    y_ref[...] = yt.T


def _kv_from_stack(stack):
    depth, n_seq, _, t = stack.shape
    return jnp.transpose(stack.reshape(depth, n_seq, N_HEADS, HEAD_DIM, t), (0, 1, 4, 2, 3))


def _kv_stack_specs(layer, seq_len):
    in_spec = pl.BlockSpec(memory_space=pl.ANY)
    out_spec = pl.BlockSpec((None, None, GROUP_W, seq_len), lambda b, i: (layer, b, 0, 0))
    return in_spec, out_spec


def _moba_prompt(q, k, v, tabs, k_stack, v_stack, layer, n_seq, seq_len):
    nb = seq_len // ATT_TILE
    nbp = -(-nb // SUBLANES) * SUBLANES
    seq = lambda b, i: (b, 0)
    fixed = lambda b, i: (0, 0)
    full = pl.BlockSpec((seq_len, GROUP_W), seq)
    tab = pl.BlockSpec((seq_len, GROUP_W), fixed)
    tile = pl.BlockSpec((ATT_TILE, GROUP_W), lambda b, i: (b * nb + i, 0))
    stack_in, stack_out = _kv_stack_specs(layer, seq_len)
    stack_sd = jax.ShapeDtypeStruct(k_stack.shape, F32)
    return pl.pallas_call(
        functools.partial(_moba_prompt_kernel, nb=nb),
        grid=(n_seq, nb),
        in_specs=[tile, full, full, tab, tab, tab, stack_in, stack_in],
        out_specs=[tile, stack_out, stack_out],
        out_shape=[jax.ShapeDtypeStruct((n_seq * seq_len, GROUP_W), F32), stack_sd, stack_sd],
        input_output_aliases={6: 1, 7: 2},
        scratch_shapes=[pltpu.VMEM((N_HEADS, seq_len, HEAD_DIM), BF16), pltpu.VMEM((N_HEADS, VT_ROWS, seq_len), BF16),
                        pltpu.VMEM((nbp, GROUP_W), F32)],
        compiler_params=_cparams(("arbitrary", "arbitrary"), 56),
        name="moba_prompt",
    )(q, k, v, *tabs, k_stack, v_stack)


def _diff_prompt_kernel(q_ref, k_ref, v_ref, c_ref, sa_ref, sb_ref, dp_ref, sw_ref, kstack_ref, vstack_ref,
                        y_ref, kt_out_ref, vt_out_ref, k_scr, vt_scr, *, lam_init):
    del kstack_ref, vstack_ref
    i = pl.program_id(1)
    tq = ATT_TILE
    half = DIFF_DH // 8

    @pl.when(i == 0)
    def _():
        _prep_kv(k_ref, v_ref, (c_ref, sa_ref, sb_ref), half, kt_out_ref, vt_out_ref, k_scr, vt_scr)

    start = pl.multiple_of(i * tq, tq)
    q = _rope(q_ref[...], c_ref[pl.ds(start, tq), :], sa_ref[pl.ds(start, tq), :], sb_ref[pl.ds(start, tq), :], half)
    qt = q.T
    scale = DIFF_DH ** -0.5
    dim_i = lax.broadcasted_iota(jnp.int32, (HEAD_DIM, tq), 0)

    chains = []
    for h in range(N_HEADS):
        qth = qt[h * HEAD_DIM:(h + 1) * HEAD_DIM, :] * scale
        for j in range(2):
            chains.append((h, jnp.where((dim_i >= DIFF_DH) == (j == 1), qth, 0.0).astype(BF16)))

    def body(n, carries):
        off = pl.multiple_of(n * ATT_KEYS, ATT_KEYS)
        return _attend_step(k_scr, vt_scr, chains, off, carries, [None] * len(chains))

    fin = lax.fori_loop(0, i * ATT_SUB, body, tuple(_own_block(k_scr, vt_scr, chains, start)))
    lam = _diff_lambda(dp_ref[...], lam_init)
    outs = []
    for h in range(N_HEADS):
        o = _attend_result(fin[2 * h]) - lam * _attend_result(fin[2 * h + 1])
        o = o * lax.rsqrt(jnp.mean(o * o, axis=0, keepdims=True) + RMS_EPS) * sw_ref[...]
        outs.append(o * (1.0 - lam_init))
    y_ref[...] = jnp.concatenate(outs, axis=0).T


def _diff_prompt(q, k, v, tabs, dparams, subln_w, lam_init, k_stack, v_stack, layer, n_seq, seq_len):
    nt = seq_len // ATT_TILE
    seq = lambda b, i: (b, 0)
    fixed = lambda b, i: (0, 0)
    full = pl.BlockSpec((seq_len, GROUP_W), seq)
    tab = pl.BlockSpec((seq_len, GROUP_W), fixed)
    tile = pl.BlockSpec((ATT_TILE, GROUP_W), lambda b, i: (b * nt + i, 0))
    subln_cols = jnp.broadcast_to(subln_w[:, None], (HEAD_DIM, ATT_TILE))
    stack_in, stack_out = _kv_stack_specs(layer, seq_len)
    stack_sd = jax.ShapeDtypeStruct(k_stack.shape, F32)
    return pl.pallas_call(
        functools.partial(_diff_prompt_kernel, lam_init=float(lam_init)),
        grid=(n_seq, nt),
        in_specs=[tile, full, full, tab, tab, tab, pl.BlockSpec((SUBLANES, LANES), fixed),
                  pl.BlockSpec((HEAD_DIM, ATT_TILE), fixed), stack_in, stack_in],
        out_specs=[tile, stack_out, stack_out],
        out_shape=[jax.ShapeDtypeStruct((n_seq * seq_len, GROUP_W), F32), stack_sd, stack_sd],
        input_output_aliases={8: 1, 9: 2},
        scratch_shapes=[pltpu.VMEM((N_HEADS, seq_len, HEAD_DIM), BF16), pltpu.VMEM((N_HEADS, VT_ROWS, seq_len), BF16)],
        compiler_params=_cparams(("arbitrary", "arbitrary"), 56),
        name="diff_prompt",
    )(q, k, v, *tabs, dparams, subln_cols, k_stack, v_stack)


SAMPLE_T = 8
NQ_MOBA = N_HEADS * SAMPLE_T
NQ_DIFF = 2 * N_HEADS * SAMPLE_T


def _block_diag_queries(q, n_maps):
    nq = n_maps * N_HEADS * SAMPLE_T
    tiled = jnp.concatenate([q] * (nq // SAMPLE_T) + [jnp.zeros((LANES - nq, GROUP_W), F32)], axis=0)
    row = lax.broadcasted_iota(jnp.int32, (LANES, GROUP_W), 0)
    lane = lax.broadcasted_iota(jnp.int32, (LANES, GROUP_W), 1)
    width = HEAD_DIM // n_maps
    keep = (row < nq) & ((row // SAMPLE_T) % N_HEADS == lane // HEAD_DIM) & (row // (N_HEADS * SAMPLE_T) == (lane % HEAD_DIM) // width)
    return jnp.where(keep, tiled, 0.0)


def _own_partials(k_new, v_new, qblk, scale, nq):
    pad = jnp.zeros((LANES - SAMPLE_T, GROUP_W), F32)
    kp = jnp.concatenate([k_new, pad], axis=0)
    vp = jnp.concatenate([v_new, pad], axis=0)
    s = _dot_nt(qblk[0:nq, :], kp) * scale
    key = lax.broadcasted_iota(jnp.int32, (nq, LANES), 1)
    qry = lax.broadcasted_iota(jnp.int32, (nq, LANES), 0) % SAMPLE_T
    s = jnp.where(key <= qry, s, NEG_BIG)
    m = s.max(axis=-1, keepdims=True)
    p = jnp.exp(s - m)
    return m, p.sum(axis=-1, keepdims=True), _dot(p, vp)


def _merge_cols(m_scr, l_scr, o_scr, sel, own, nblk, nq):
    m_own, l_own, o_own = own
    m_all = m_scr[0:nq, :]
    m_tot = jnp.maximum(jnp.max(jnp.where(sel, m_all, NEG_BIG), axis=-1, keepdims=True), m_own)
    w = jnp.where(sel, jnp.exp(m_all - m_tot), 0.0)
    e_own = jnp.exp(m_own - m_tot)
    den = jnp.sum(w * l_scr[0:nq, :], axis=-1, keepdims=True) + e_own * l_own
    num = e_own * o_own
    for n in range(nblk):
        num += w[:, n:n + 1] * o_scr[n]
    return num / den


def _sample_attn_kernel(pt_ref, mq_ref, mk_ref, mv_ref, dq_ref, dk_ref, dv_ref, *rest, nblk, per_blk, bps, lam_init):
    per_step = per_blk * bps
    n_pages = 4 * per_step
    page_refs = rest[:n_pages]
    (mc_ref, msa_ref, msb_ref, dc_ref, dsa_ref, dsb_ref, dp_ref,
     ym_ref, yd_ref, mkout_ref, dkout_ref,
     qm_scr, qd_scr, g_scr, mm_scr, lm_scr, om_scr, md_scr, ld_scr, od_scr) = rest[n_pages:]
    n = pl.program_id(1)
    m_half = HEAD_DIM // 8
    d_half = DIFF_DH // 8
    m_scale = HEAD_DIM ** -0.5
    d_scale = DIFF_DH ** -0.5

    @pl.when(n == 0)
    def _():
        qm = _rope(mq_ref[...], mc_ref[...], msa_ref[...], msb_ref[...], m_half)
        qm_scr[...] = _block_diag_queries(qm, 1)
        qd = _rope(dq_ref[...], dc_ref[...], dsa_ref[...], dsb_ref[...], d_half)
        qd_scr[...] = _block_diag_queries(qd, 2)
        for scr in (g_scr, mm_scr, lm_scr, md_scr, ld_scr):
            scr[...] = jnp.zeros_like(scr)

    def pages(cache_idx, i):
        base = cache_idx * per_step + i * per_blk
        return [page_refs[base + j][...].reshape(GROUP_W, LANES) for j in range(per_blk)]

    lane_m = lax.broadcasted_iota(jnp.int32, (NQ_MOBA, LANES), 1)
    lane_d = lax.broadcasted_iota(jnp.int32, (NQ_DIFF, LANES), 1)
    qm_blk = qm_scr[...]
    qd_blk = qd_scr[...]
    mm, lm, gm = mm_scr[0:NQ_MOBA, :], lm_scr[0:NQ_MOBA, :], g_scr[0:NQ_MOBA, :]
    md, ld = md_scr[0:NQ_DIFF, :], ld_scr[0:NQ_DIFF, :]
    units = [(i, kind) for i in range(bps) for kind in (0, 1)]
    qbs = (qm_blk[0:NQ_MOBA, :].astype(BF16), qd_blk[0:NQ_DIFF, :].astype(BF16))
    scales = (m_scale, d_scale)
    kts = {(i, kind): pages(2 * kind, i) for i, kind in units}
    vts = {(i, kind): pages(2 * kind + 1, i) for i, kind in units}
    scores = {u: [_dot(qbs[u[1]], kt) * scales[u[1]] for kt in kts[u]] for u in units}
    k_means = [sum(kts[(i, 0)]).sum(axis=-1, keepdims=True) * (1.0 / MOBA_BLOCK) for i in range(bps)]
    gates = [_dot(qm_blk[0:NQ_MOBA, :], jnp.broadcast_to(km, (GROUP_W, LANES)), HI) for km in k_means]
    probs, stats = {}, {}
    for u in units:
        m = scores[u][0].max(axis=-1, keepdims=True)
        for s in scores[u][1:]:
            m = jnp.maximum(m, s.max(axis=-1, keepdims=True))
        probs[u] = [jnp.exp(s - m) for s in scores[u]]
        stats[u] = (m, sum(p.sum(axis=-1, keepdims=True) for p in probs[u]))
    outs = {u: sum(_dot_nt(p, vt) for p, vt in zip(probs[u], vts[u])) for u in units}
    for i in range(bps):
        blk = n * bps + i
        mm = jnp.where(lane_m == blk, stats[(i, 0)][0], mm)
        lm = jnp.where(lane_m == blk, stats[(i, 0)][1], lm)
        gm = jnp.where(lane_m == blk, gates[i], gm)
        md = jnp.where(lane_d == blk, stats[(i, 1)][0], md)
        ld = jnp.where(lane_d == blk, stats[(i, 1)][1], ld)
        om_scr[blk] = outs[(i, 0)]
        od_scr[blk] = outs[(i, 1)]
    mm_scr[0:NQ_MOBA, :], lm_scr[0:NQ_MOBA, :], g_scr[0:NQ_MOBA, :] = mm, lm, gm
    md_scr[0:NQ_DIFF, :], ld_scr[0:NQ_DIFF, :] = md, ld

    @pl.when(n == nblk // bps - 1)
    def _():
        k_new = _rope(mk_ref[...], mc_ref[...], msa_ref[...], msb_ref[...], m_half)
        mkout_ref[...] = k_new
        own = _own_partials(k_new, mv_ref[...], qm_scr[...], m_scale, NQ_MOBA)
        gate = jnp.where(lane_m < nblk, g_scr[0:NQ_MOBA, :], -jnp.inf)
        cnt = jnp.zeros((NQ_MOBA, LANES), F32)
        for j in range(nblk):
            gj = gate[:, j:j + 1]
            cnt += jnp.where((gj > gate) | ((gj == gate) & (j < lane_m)), 1.0, 0.0)
        sel = (lane_m < nblk) & (cnt < MOBA_TOPK)
        rows_m = _merge_cols(mm_scr, lm_scr, om_scr, sel, own, nblk, NQ_MOBA)
        ym_ref[...] = jnp.concatenate(
            [rows_m[h * SAMPLE_T:(h + 1) * SAMPLE_T, h * HEAD_DIM:(h + 1) * HEAD_DIM] for h in range(N_HEADS)], axis=-1)

        k_new = _rope(dk_ref[...], dc_ref[...], dsa_ref[...], dsb_ref[...], d_half)
        dkout_ref[...] = k_new
        own = _own_partials(k_new, dv_ref[...], qd_scr[...], d_scale, NQ_DIFF)
        rows_d = _merge_cols(md_scr, ld_scr, od_scr, lane_d < nblk, own, nblk, NQ_DIFF)
        dp = dp_ref[...]
        lam = _diff_lambda(dp, lam_init)
        subln = dp[4:5, 0:2 * DIFF_DH]
        outs = []
        for h in range(N_HEADS):
            o1 = rows_d[h * SAMPLE_T:(h + 1) * SAMPLE_T, h * HEAD_DIM:(h + 1) * HEAD_DIM]
            o2 = rows_d[NQ_MOBA + h * SAMPLE_T:NQ_MOBA + (h + 1) * SAMPLE_T, h * HEAD_DIM:(h + 1) * HEAD_DIM]
            outs.append(_rms(o1 - lam * o2, subln) * (1.0 - lam_init))
        yd_ref[...] = jnp.concatenate(outs, axis=-1)


SAMPLE_BLOCKS_PER_STEP = 8


def _sample_attn(mq, mk, mv, dq, dk, dv, caches_t, page_table, tabs_m, tabs_d, dparams, lam_init, layer, n_seq):
    n_pages = page_table.shape[1]
    page = caches_t[0].shape[-1]
    per_blk = MOBA_BLOCK // page
    nblk = n_pages // per_blk
    assert page == LANES and nblk <= LANES
    bps = math.gcd(nblk, SAMPLE_BLOCKS_PER_STEP)
    per_step = per_blk * bps
    new = pl.BlockSpec((SAMPLE_T, GROUP_W), lambda b, n, pt: (b, 0))
    tab = pl.BlockSpec((SAMPLE_T, GROUP_W), lambda b, n, pt: (0, 0))

    def page_spec(j):
        return pl.BlockSpec((None, None, N_HEADS, HEAD_DIM, page),
                            lambda b, n, pt: (layer, pt[b, per_step * n + j], 0, 0, 0))

    cache_specs, cache_args = [], []
    for pool in caches_t:
        for j in range(per_step):
            cache_specs.append(page_spec(j))
            cache_args.append(pool)
    out_sd = jax.ShapeDtypeStruct((n_seq * SAMPLE_T, GROUP_W), F32)
    stat = pltpu.VMEM((LANES, LANES), F32)
    grid_spec = pltpu.PrefetchScalarGridSpec(
        num_scalar_prefetch=1,
        grid=(n_seq, nblk // bps),
        in_specs=[new] * 6 + cache_specs + [tab] * 6 + [pl.BlockSpec((SUBLANES, LANES), lambda b, n, pt: (0, 0))],
        out_specs=[new] * 4,
        scratch_shapes=[pltpu.VMEM((LANES, GROUP_W), F32), pltpu.VMEM((LANES, GROUP_W), F32),
                        stat, stat, stat, pltpu.VMEM((nblk, NQ_MOBA, GROUP_W), F32),
                        stat, stat, pltpu.VMEM((nblk, NQ_DIFF, GROUP_W), F32)],
    )
    return pl.pallas_call(
        functools.partial(_sample_attn_kernel, nblk=nblk, per_blk=per_blk, bps=bps, lam_init=float(lam_init)),
        grid_spec=grid_spec,
        out_shape=[out_sd] * 4,
        compiler_params=_cparams(("arbitrary", "arbitrary"), 32),
        name="sample_attn",
    )(page_table, mq, mk, mv, dq, dk, dv, *cache_args, *tabs_m, *tabs_d, dparams)


def _reorder_w_in(w_in):
    o_a = GROUP_W + SSM_XBC + N_HEADS
    xbc = w_in[..., GROUP_W:GROUP_W + SSM_XBC]
    z = w_in[..., 0:GROUP_W]
    dt = jnp.pad(w_in[..., GROUP_W + SSM_XBC:o_a], ((0, 0), (0, 0), (0, LANES - N_HEADS)))
    rest = w_in[..., o_a:]
    return jnp.concatenate([xbc, z, rest, dt], axis=-1)


def kernel(x_prompt, x_sample, c_prompt, c_sample, cache_moba_k, cache_moba_v, cache_diff_k, cache_diff_v, page_table, state_ssm, state_conv, state_wkv, state_shift, w_ada, b_ada, norm_mix_pre, norm_mix_post, norm_ffn_pre, norm_ffn_post, w_in, w_out, conv_w, conv_b, dt_bias, a_log, d_skip, ssm_norm_w, lambda_q1, lambda_k1, lambda_q2, lambda_k2, subln_w, mu_shift, w0, w2, a0, a2, g2, k_k, k_a, r_k, ln_x_w, ln_x_b, w_gate, w_up, w_down):
    nbp, tp, d = x_prompt.shape
    nbs, ts, _ = x_sample.shape
    depth = w_in.shape[0]
    assert ts == SAMPLE_T and d == D_MODEL and tp % MOBA_BLOCK == 0
    past_len = page_table.shape[1] * cache_moba_k.shape[2]
    assert past_len % MOBA_BLOCK == 0

    w_in_b = _reorder_w_in(w_in).astype(BF16)
    w_out_b, w_gate_b, w_up_b, w_down_b = (w.astype(BF16) for w in (w_out, w_gate, w_up, w_down))
    mods = _ada_mod(jnp.concatenate([c_prompt, c_sample], axis=0), w_ada, b_ada)
    norm_pre = norm_mix_pre.reshape(depth, 1, d)
    norms3 = jnp.stack([norm_mix_post, norm_ffn_pre, norm_ffn_post], axis=1).reshape(depth, 3, 1, d)
    pos_p = jnp.arange(tp)
    pos_s = past_len + jnp.arange(ts)
    tabs_mp, tabs_dp = _rope_tables(pos_p, HEAD_DIM), _rope_tables(pos_p, DIFF_DH)
    tabs_ms, tabs_ds = _rope_tables(pos_s, HEAD_DIM), _rope_tables(pos_s, DIFF_DH)
    caches = [jnp.transpose(c, (0, 1, 3, 4, 2)) for c in (cache_moba_k, cache_moba_v, cache_diff_k, cache_diff_v)]

    xp = x_prompt.reshape(nbp * tp, d)
    xs = x_sample.reshape(nbs * ts, d)
    zeros_conv = jnp.zeros((nbp, SUBLANES, SSM_XBC), F32)
    zeros_ssm = jnp.zeros((nbp, N_HEADS, HEAD_DIM, SSM_STATE), F32)
    zeros_shift = jnp.zeros((nbp, 1, RWKV_PROJ), F32)
    zeros_wkv = jnp.zeros((nbp, N_HEADS, HEAD_DIM, HEAD_DIM), F32)
    kv_p = [jnp.zeros((depth, nbp, GROUP_W, tp), F32) for _ in range(4)]
    new_p, new_s = [], []
    for l in range(depth):
        lam_init = 0.8 - 0.6 * math.exp(-0.3 * l)
        head_params = _ssd_head_params(dt_bias[l], a_log[l], d_skip[l])
        vec_params = _rwkv_vec_params(dict(mu_shift=mu_shift[l], w0=w0[l], a0=a0[l], k_k=k_k[l], k_a=k_a[l],
                                           r_k=r_k[l], ln_x_w=ln_x_w[l], ln_x_b=ln_x_b[l]))
        dparams = _diff_params(dict(lambda_q1=lambda_q1[l], lambda_k1=lambda_k1[l], lambda_q2=lambda_q2[l],
                                    lambda_k2=lambda_k2[l], subln_w=subln_w[l]))
        ssd_w = (conv_w[l], conv_b[l][None], head_params, ssm_norm_w[l][None])
        rwkv_w = (vec_params, w2[l], a2[l], g2[l])
        dense_w = (norms3, w_out_b, w_gate_b, w_up_b, w_down_b)

        mod_p = mods[l, :nbp].reshape(nbp, N_MOD, 1, d)
        pr = _in_proj(xp, mod_p, norm_pre, w_in_b, l, ROW_TILE)
        ya, conv_p, ssm_p = _ssd(pr['xbc'], pr['z'], pr['dt'], zeros_conv, zeros_ssm, *ssd_w, nbp, tp)
        yb, kv_p[0], kv_p[1] = _moba_prompt(pr['mq'], pr['mk'], pr['mv'], tabs_mp, kv_p[0], kv_p[1], l, nbp, tp)
        yc, kv_p[2], kv_p[3] = _diff_prompt(pr['dq'], pr['dk'], pr['dv'], tabs_dp, dparams, subln_w[l], lam_init,
                                            kv_p[2], kv_p[3], l, nbp, tp)
        yd, shift_p, wkv_p = _rwkv(pr['pd'], zeros_shift, zeros_wkv, *rwkv_w, nbp, tp)
        xp = _post(xp, (ya, yb, yc, yd), mod_p, *dense_w, l, ROW_TILE, FF_TILE)
        new_p.append((ssm_p, conv_p, wkv_p, shift_p))

        mod_s = jnp.repeat(mods[l, nbp:].reshape(nbs, N_MOD, d), ts, axis=0).transpose(1, 0, 2)[None]
        sr = _in_proj(xs, mod_s, norm_pre, w_in_b, l, nbs * ts)
        conv_prev8 = jnp.pad(state_conv[l], ((0, 0), (SUBLANES - (SSM_CONV - 1), 0), (0, 0)))
        ya, conv_s, ssm_s = _ssd(sr['xbc'], sr['z'], sr['dt'], conv_prev8, state_ssm[l], *ssd_w, nbs, ts)
        yb, yc, mk_s, dk_s = _sample_attn(sr['mq'], sr['mk'], sr['mv'], sr['dq'], sr['dk'], sr['dv'], caches,
                                          page_table, tabs_ms, tabs_ds, dparams, lam_init, l, nbs)
        yd, shift_s, wkv_s = _rwkv(sr['pd'], state_shift[l][:, None, :], state_wkv[l], *rwkv_w, nbs, ts)
        xs = _post(xs, (ya, yb, yc, yd), mod_s, *dense_w, l, nbs * ts, FF_TILE)
        new_s.append((mk_s, sr['mv'], dk_s, sr['dv'], ssm_s, conv_s, wkv_s, shift_s))

    def stack(states, i):
        return jnp.stack([s[i] for s in states])

    kv_s_shape = (depth, nbs, ts, N_HEADS, HEAD_DIM)
    out_p = tuple(_kv_from_stack(a) for a in kv_p) + tuple(stack(new_p, i) for i in range(4))
    out_s = tuple(stack(new_s, i).reshape(kv_s_shape) for i in range(4)) + tuple(stack(new_s, i) for i in range(4, 8))
    return (xp.reshape(nbp, tp, d), xs.reshape(nbs, ts, d)) + out_p + out_s
```

```python
import functools
import math

import jax
import jax.numpy as jnp
from jax import lax
from jax.experimental import pallas as pl
from jax.experimental.pallas import tpu as pltpu

F32 = jnp.float32
BF16 = jnp.bfloat16
HI = lax.Precision.HIGHEST

V7X_VMEM_BYTES = 64 * 1024 * 1024
LANES = 128
SUBLANES = 8

D_MODEL = 1024
GROUP_W = 256
HEAD_DIM = 64
N_HEADS = 4
SSM_STATE = 128
SSM_CONV = 4
SSM_CHUNK = 128
SSM_XBC = 768
MOBA_BLOCK = 256
MOBA_TOPK = 3
DIFF_DH = 32
RWKV_PROJ = 1024
RWKV_CHUNK = 64
RWKV_LN_EPS = 64e-5
ROPE_THETA = 500000.0
D_FF = 2816
N_MOD = 6
RMS_EPS = 1e-6
NEG_BIG = -1e30

SEG_WIDTHS = (("xbc", 768), ("z", 256), ("mq", 256), ("mk", 256), ("mv", 256), ("dq", 256), ("dk", 256),
              ("dv", 256), ("pd", 1024), ("dt", 128))
IN_PROJ_PAD = sum(w for _, w in SEG_WIDTHS)
ROW_TILE = 512
FF_TILE = 1408


def _cparams(sem, vmem_mib):
    return pltpu.CompilerParams(dimension_semantics=sem, vmem_limit_bytes=vmem_mib * 1024 * 1024)


def _sigmoid(x):
    return 1.0 / (1.0 + jnp.exp(-x))


def _silu(x):
    return x * _sigmoid(x)


def _softplus(x):
    return jnp.maximum(x, 0.0) + jnp.log(1.0 + jnp.exp(-jnp.abs(x)))


def _mxu_operands(a, b, precision):
    if precision is None:
        return a.astype(BF16), b.astype(BF16)
    return a, b


def _dot(a, b, precision=None):
    a, b = _mxu_operands(a, b, precision)
    return jnp.dot(a, b, preferred_element_type=F32, precision=precision)


def _dot_nt(a, b, precision=None):
    a, b = _mxu_operands(a, b, precision)
    return lax.dot_general(a, b, (((1,), (1,)), ((), ())), preferred_element_type=F32, precision=precision)


def _dot_tn(a, b, precision=None):
    a, b = _mxu_operands(a, b, precision)
    return lax.dot_general(a, b, (((0,), (0,)), ((), ())), preferred_element_type=F32, precision=precision)


def _rms(x, w):
    return x * lax.rsqrt(jnp.mean(x * x, axis=-1, keepdims=True) + RMS_EPS) * w


def _mod_kernel(c_ref, w_ref, b_ref, o_ref):
    c = c_ref[...]
    o_ref[...] = _dot(_silu(c).astype(BF16), w_ref[...].astype(BF16)) + b_ref[...]


def _ada_mod(c_all, w_ada, b_ada):
    depth, _, n = w_ada.shape
    nb = c_all.shape[0]
    tn = 1536
    return pl.pallas_call(
        _mod_kernel,
        grid=(depth, n // tn),
        in_specs=[pl.BlockSpec((nb, D_MODEL), lambda l, j: (0, 0)),
                  pl.BlockSpec((None, D_MODEL, tn), lambda l, j: (l, 0, j)),
                  pl.BlockSpec((None, 1, tn), lambda l, j: (l, 0, j))],
        out_specs=pl.BlockSpec((None, nb, tn), lambda l, j: (l, 0, j)),
        out_shape=jax.ShapeDtypeStruct((depth, nb, n), F32),
        compiler_params=_cparams(("arbitrary", "arbitrary"), 40),
        name="ada_mod",
    )(c_all, w_ada, b_ada.reshape(depth, 1, n))


def _inproj_kernel(x_ref, mod_ref, nw_ref, w_ref, *out_refs):
    x = x_ref[...]
    h = _rms(x, nw_ref[...]) * (1.0 + mod_ref[1]) + mod_ref[0]
    hb = h.astype(BF16)
    off = 0
    for o_ref, (_, width) in zip(out_refs, SEG_WIDTHS):
        o_ref[...] = _dot(hb, w_ref[:, off:off + width])
        off += width


def _in_proj(x2d, mod4, norm_w, w_in_b, layer, tm):
    n_tok = x2d.shape[0]
    nb, _, r, _ = mod4.shape
    tiles_per_mod = (n_tok // tm) // nb
    out_shape = [jax.ShapeDtypeStruct((n_tok, w), F32) for _, w in SEG_WIDTHS]
    out_specs = [pl.BlockSpec((tm, w), lambda i: (i, 0)) for _, w in SEG_WIDTHS]
    outs = pl.pallas_call(
        _inproj_kernel,
        grid=(n_tok // tm,),
        in_specs=[pl.BlockSpec((tm, D_MODEL), lambda i: (i, 0)),
                  pl.BlockSpec((None, N_MOD, r, D_MODEL), lambda i: (i // tiles_per_mod, 0, 0, 0)),
                  pl.BlockSpec((None, 1, D_MODEL), lambda i: (layer, 0, 0)),
                  pl.BlockSpec((None, D_MODEL, IN_PROJ_PAD), lambda i: (layer, 0, 0))],
        out_specs=out_specs,
        out_shape=out_shape,
        compiler_params=_cparams(("arbitrary",), 56),
        name="in_proj",
    )(x2d, mod4, norm_w, w_in_b)
    return dict(zip([n for n, _ in SEG_WIDTHS], outs))


def _post_kernel(x_ref, ya_ref, yb_ref, yc_ref, yd_ref, mod_ref, nw_ref, wo_ref, wg_ref, wu_ref, wd_ref,
                 o_ref, x1_scr, h_scr, acc_scr):
    f = pl.program_id(1)

    @pl.when(f == 0)
    def _():
        mixed = _dot(ya_ref[...].astype(BF16), wo_ref[0:GROUP_W, :])
        mixed += _dot(yb_ref[...].astype(BF16), wo_ref[GROUP_W:2 * GROUP_W, :])
        mixed += _dot(yc_ref[...].astype(BF16), wo_ref[2 * GROUP_W:3 * GROUP_W, :])
        mixed += _dot(yd_ref[...].astype(BF16), wo_ref[3 * GROUP_W:4 * GROUP_W, :])
        x1 = x_ref[...] + (1.0 + mod_ref[2]) * _rms(mixed, nw_ref[0])
        x1_scr[...] = x1
        h = _rms(x1, nw_ref[1]) * (1.0 + mod_ref[4]) + mod_ref[3]
        h_scr[...] = h.astype(BF16)
        acc_scr[...] = jnp.zeros_like(acc_scr)

    hb = h_scr[...]
    a = _silu(_dot(hb, wg_ref[...])) * _dot(hb, wu_ref[...])
    acc_scr[...] += _dot(a.astype(BF16), wd_ref[...])

    @pl.when(f == pl.num_programs(1) - 1)
    def _():
        o_ref[...] = x1_scr[...] + (1.0 + mod_ref[5]) * _rms(acc_scr[...], nw_ref[2])


def _post(x2d, ys, mod4, norms3, w_out_b, w_gate_b, w_up_b, w_down_b, layer, tm, tf):
    n_tok = x2d.shape[0]
    nb, _, r, _ = mod4.shape
    tiles_per_mod = (n_tok // tm) // nb
    yspec = pl.BlockSpec((tm, GROUP_W), lambda i, f: (i, 0))
    return pl.pallas_call(
        _post_kernel,
        grid=(n_tok // tm, D_FF // tf),
        in_specs=[pl.BlockSpec((tm, D_MODEL), lambda i, f: (i, 0)), yspec, yspec, yspec, yspec,
                  pl.BlockSpec((None, N_MOD, r, D_MODEL), lambda i, f: (i // tiles_per_mod, 0, 0, 0)),
                  pl.BlockSpec((None, 3, 1, D_MODEL), lambda i, f: (layer, 0, 0, 0)),
                  pl.BlockSpec((None, D_MODEL, D_MODEL), lambda i, f: (layer, 0, 0)),
                  pl.BlockSpec((None, D_MODEL, tf), lambda i, f: (layer, 0, f)),
                  pl.BlockSpec((None, D_MODEL, tf), lambda i, f: (layer, 0, f)),
                  pl.BlockSpec((None, tf, D_MODEL), lambda i, f: (layer, f, 0))],
        out_specs=pl.BlockSpec((tm, D_MODEL), lambda i, f: (i, 0)),
        out_shape=jax.ShapeDtypeStruct((n_tok, D_MODEL), F32),
        scratch_shapes=[pltpu.VMEM((tm, D_MODEL), F32), pltpu.VMEM((tm, D_MODEL), BF16),
                        pltpu.VMEM((tm, D_MODEL), F32)],
        compiler_params=_cparams(("arbitrary", "arbitrary"), 56),
        name="post",
    )(x2d, ys[0], ys[1], ys[2], ys[3], mod4, norms3, w_out_b, w_gate_b, w_up_b, w_down_b)


def _ssd_kernel(xbc_ref, z_ref, dt_ref, cprev_ref, sprev_ref, cw_ref, cb_ref, hp_ref, nw_ref,
                y_ref, cnew_ref, snew_ref, buf, st, *, q, g_chunks):
    c = pl.program_id(1)
    n = q * g_chunks

    @pl.when(c == 0)
    def _():
        buf[0:SUBLANES, :] = cprev_ref[...]
        st[...] = sprev_ref[...]

    xin = xbc_ref[...]
    buf[SUBLANES:SUBLANES + n, :] = xin
    conv = cb_ref[...] + cw_ref[3:4, :] * xin
    for i in range(SSM_CONV - 1):
        conv += cw_ref[i:i + 1, :] * buf[SUBLANES - 3 + i:SUBLANES - 3 + i + n, :]
    tail = buf[n:n + SUBLANES, :]
    buf[0:SUBLANES, :] = tail
    cnew_ref[...] = tail

    xc = _silu(conv)
    xs = xc[:, 0:GROUP_W]
    bm = xc[:, GROUP_W:GROUP_W + 2 * SSM_STATE]
    cm = xc[:, GROUP_W + 2 * SSM_STATE:]
    hp = hp_ref[...]
    dt = _softplus(dt_ref[...] + hp[0:1, :])
    cs = _cumsum_rows(dt * (-jnp.exp(hp[1:2, :])), q)
    rows = lax.broadcasted_iota(jnp.int32, (q, q), 0)
    cols = lax.broadcasted_iota(jnp.int32, (q, q), 1)
    tril = rows >= cols
    onehot = (lax.broadcasted_iota(jnp.int32, (SUBLANES, LANES), 0)
              == lax.broadcasted_iota(jnp.int32, (SUBLANES, LANES), 1)).astype(F32)

    chunks = range(g_chunks)
    units = [(gi, hd) for gi in chunks for hd in range(N_HEADS)]
    rsl = [slice(gi * q, (gi + 1) * q) for gi in chunks]
    cs_t = [_dot_nt(onehot, cs[rsl[gi], :], HI) for gi in chunks]
    bgs = {(gi, g): bm[rsl[gi], g * SSM_STATE:(g + 1) * SSM_STATE] for gi in chunks for g in range(2)}
    cgs = {(gi, g): cm[rsl[gi], g * SSM_STATE:(g + 1) * SSM_STATE] for gi in chunks for g in range(2)}
    cbs = {k: _dot_nt(cgs[k], bgs[k]) for k in bgs}
    cs_col = {(gi, hd): cs[rsl[gi], hd:hd + 1] for gi, hd in units}
    cs_last = {(gi, hd): cs[(gi + 1) * q - 1:(gi + 1) * q, hd:hd + 1] for gi, hd in units}
    decay_in = {u: jnp.where(tril, jnp.exp(jnp.where(tril, cs_col[u] - cs_t[u[0]][u[1]:u[1] + 1, :], 0.0)), 0.0)
                for u in units}
    xh = {(gi, hd): xs[rsl[gi], hd * HEAD_DIM:(hd + 1) * HEAD_DIM] for gi, hd in units}
    xdt = {(gi, hd): xh[(gi, hd)] * dt[rsl[gi], hd:hd + 1] for gi, hd in units}
    y_diag = {(gi, hd): _dot(cbs[(gi, hd // 2)] * decay_in[(gi, hd)], xdt[(gi, hd)]) for gi, hd in units}
    s_upd = {(gi, hd): _dot_tn(xdt[(gi, hd)] * jnp.exp(cs_last[(gi, hd)] - cs_col[(gi, hd)]), bgs[(gi, hd // 2)])
             for gi, hd in units}
    s_in = {}
    for hd in range(N_HEADS):
        s_cur = st[hd]
        for gi in chunks:
            s_in[(gi, hd)] = s_cur
            s_cur = jnp.exp(cs_last[(gi, hd)]) * s_cur + s_upd[(gi, hd)]
        st[hd] = s_cur
    y_off = {(gi, hd): _dot_nt(cgs[(gi, hd // 2)], s_in[(gi, hd)]) for gi, hd in units}
    y_rows = []
    for gi in chunks:
        y_rows.append(jnp.concatenate(
            [y_diag[(gi, hd)] + y_off[(gi, hd)] * jnp.exp(cs_col[(gi, hd)]) + hp[2:3, hd:hd + 1] * xh[(gi, hd)]
             for hd in range(N_HEADS)], axis=-1))
    y = (jnp.concatenate(y_rows, axis=0) if g_chunks > 1 else y_rows[0]) * _silu(z_ref[...])
    y_ref[...] = _rms(y, nw_ref[...])

    @pl.when(c == pl.num_programs(1) - 1)
    def _():
        snew_ref[...] = st[...]


def _ssd_head_params(dt_bias, a_log, d_skip):
    rows = jnp.stack([dt_bias, a_log, d_skip]).astype(F32)
    return jnp.pad(rows, ((0, SUBLANES - 3), (0, LANES - N_HEADS)))


def _rwkv_vec_params(p):
    small = jnp.stack([p['w0'], p['a0'], p['k_k'], p['k_a'], p['r_k'].reshape(GROUP_W), p['ln_x_w'], p['ln_x_b']])
    return jnp.concatenate([p['mu_shift'][None, :], jnp.pad(small, ((0, 0), (0, RWKV_PROJ - GROUP_W)))], axis=0)


SSD_CHUNKS_PER_STEP = 4


def _ssd(xbc, z, dt, conv_prev8, ssm_prev, conv_w, conv_b, head_params, norm_w, n_seq, seq_len):
    q = math.gcd(seq_len, SSM_CHUNK)
    g_chunks = math.gcd(seq_len // q, SSD_CHUNKS_PER_STEP)
    n = q * g_chunks
    nc = seq_len // n
    tok = lambda b, c: (b * nc + c, 0)
    fixed2 = lambda b, c: (0, 0)
    y, conv_new8, ssm_new = pl.pallas_call(
        functools.partial(_ssd_kernel, q=q, g_chunks=g_chunks),
        grid=(n_seq, nc),
        in_specs=[pl.BlockSpec((n, SSM_XBC), tok), pl.BlockSpec((n, GROUP_W), tok), pl.BlockSpec((n, LANES), tok),
                  pl.BlockSpec((None, SUBLANES, SSM_XBC), lambda b, c: (b, 0, 0)),
                  pl.BlockSpec((None, N_HEADS, HEAD_DIM, SSM_STATE), lambda b, c: (b, 0, 0, 0)),
                  pl.BlockSpec((SSM_CONV, SSM_XBC), fixed2), pl.BlockSpec((1, SSM_XBC), fixed2),
                  pl.BlockSpec((SUBLANES, LANES), fixed2), pl.BlockSpec((1, GROUP_W), fixed2)],
        out_specs=[pl.BlockSpec((n, GROUP_W), tok),
                   pl.BlockSpec((None, SUBLANES, SSM_XBC), lambda b, c: (b, 0, 0)),
                   pl.BlockSpec((None, N_HEADS, HEAD_DIM, SSM_STATE), lambda b, c: (b, 0, 0, 0))],
        out_shape=[jax.ShapeDtypeStruct((n_seq * seq_len, GROUP_W), F32),
                   jax.ShapeDtypeStruct((n_seq, SUBLANES, SSM_XBC), F32),
                   jax.ShapeDtypeStruct((n_seq, N_HEADS, HEAD_DIM, SSM_STATE), F32)],
        scratch_shapes=[pltpu.VMEM((n + SUBLANES, SSM_XBC), F32), pltpu.VMEM((N_HEADS, HEAD_DIM, SSM_STATE), F32)],
        compiler_params=_cparams(("arbitrary", "arbitrary"), 32),
        name="ssd",
    )(xbc, z, dt, conv_prev8, ssm_prev, conv_w, conv_b, head_params, norm_w)
    return y, conv_new8[:, SUBLANES - (SSM_CONV - 1):, :], ssm_new


def _cumsum_rows(x, seg):
    idx = lax.broadcasted_iota(jnp.int32, x.shape, 0) % seg
    s = 1
    while s < seg:
        x = x + jnp.where(idx >= s, pltpu.roll(x, s, 0), 0.0)
        s *= 2
    return x


def _rwkv_kernel(u_ref, shprev_ref, sprev_ref, vp_ref, w2_ref, a2_ref, g2_ref,
                  y_ref, shnew_ref, snew_ref, buf, st, *, c, g_chunks):
    ci = pl.program_id(1)
    n = c * g_chunks

    @pl.when(ci == 0)
    def _():
        buf[0:SUBLANES, :] = jnp.broadcast_to(shprev_ref[...], (SUBLANES, RWKV_PROJ))
        st[...] = sprev_ref[...]

    u = u_ref[...]
    buf[SUBLANES:SUBLANES + n, :] = u
    u_prev = buf[SUBLANES - 1:SUBLANES - 1 + n, :]
    buf[0:SUBLANES, :] = buf[n:n + SUBLANES, :]
    shnew_ref[...] = u[n - 1:n, :]

    vp = vp_ref[...]
    us = u + vp[0:1, :] * (u_prev - u)
    gw = GROUP_W
    r = us[:, 0:gw]
    k = us[:, gw:2 * gw]
    v = us[:, 2 * gw:3 * gw]
    wd = us[:, 3 * gw:3 * gw + 64]
    ad = us[:, 3 * gw + 64:3 * gw + 128]
    gd = us[:, 3 * gw + 128:]
    w0, a0, k_k, k_a = vp[1:2, 0:gw], vp[2:3, 0:gw], vp[3:4, 0:gw], vp[4:5, 0:gw]
    r_k, ln_w, ln_b = vp[5:6, 0:gw], vp[6:7, 0:gw], vp[7:8, 0:gw]
    w = -_softplus(-(w0 + _dot(jnp.tanh(wd), w2_ref[...]))) - 0.5
    a = _sigmoid(a0 + _dot(ad, a2_ref[...]))
    gate = _dot(_sigmoid(gd), g2_ref[...])
    kk_raw = k * k_k
    k2 = k * (1.0 + (a - 1.0) * k_a)
    lw = -jnp.exp(w)
    cum = _cumsum_rows(lw, c)
    e_in = jnp.exp(cum)
    e_out = jnp.exp(-cum)
    e_prev = jnp.exp(cum - lw)

    rows = lax.broadcasted_iota(jnp.int32, (c, c), 0)
    cols = lax.broadcasted_iota(jnp.int32, (c, c), 1)
    tril = rows >= cols
    stril = rows > cols
    eye = (rows == cols).astype(F32)
    units =[(gi, h) for gi in range(g_chunks) for h in range(N_HEADS)]

    def part(x, gi, h):
        return x[gi * c:(gi + 1) * c, h * HEAD_DIM:(h + 1) * HEAD_DIM]

    head_ones = (lax.broadcasted_iota(jnp.int32, (gw, gw), 0) // HEAD_DIM
                 == lax.broadcasted_iota(jnp.int32, (gw, gw), 1) // HEAD_DIM).astype(BF16)

    def head_sums(x):
        return _dot(x, head_ones)

    kk = kk_raw * lax.rsqrt(jnp.maximum(head_sums(kk_raw * kk_raw), 1e-24))
    r_t_all = r * e_in
    k_t_all = k2 * e_out
    b_t_all = kk * a * e_out
    kap_t_all = kk * e_prev

    r_t = {q: part(r_t_all, *q) for q in units}
    k_t = {q: part(k_t_all, *q) for q in units}
    b_t = {q: part(b_t_all, *q) for q in units}
    kap_t = {q: part(kap_t_all, *q) for q in units}
    vv = {q: part(v, *q) for q in units}
    aa = {q: _dot_nt(jnp.concatenate([kap_t[q], r_t[q]], axis=0), jnp.concatenate([k_t[q], b_t[q]], axis=0))
          for q in units}
    a_kb = {q: jnp.where(stril, aa[q][0:c, c:2 * c], 0.0) for q in units}
    a_v = {q: jnp.concatenate([jnp.where(stril, aa[q][0:c, 0:c], 0.0), jnp.where(tril, aa[q][c:2 * c, 0:c], 0.0)],
                              axis=0) for q in units}
    a_rb = {q: jnp.where(tril, aa[q][c:2 * c, c:2 * c], 0.0) for q in units}
    base = min(RWKV_INV_BASE, c)

    def same_block(size):
        return (rows // size) == (cols // size)

    n_pow = {q: jnp.where(same_block(base), a_kb[q], 0.0) for q in units}
    t_inv = {q: eye - n_pow[q] for q in units}
    for _ in range(int(math.log2(base)) - 1):
        n_pow = {q: _dot(n_pow[q], n_pow[q]) for q in units}
        t_inv = {q: t_inv[q] + _dot(t_inv[q], n_pow[q]) for q in units}
    size = base
    while size < c:
        off = same_block(2 * size) & jnp.logical_not(same_block(size))
        ct = {q: _dot(jnp.where(off, a_kb[q], 0.0), t_inv[q]) for q in units}
        t_inv = {q: t_inv[q] - _dot(t_inv[q], ct[q]) for q in units}
        size *= 2
    av = {q: _dot(a_v[q], vv[q]) for q in units}
    wu = {q: _dot(t_inv[q], jnp.concatenate([kap_t[q], av[q][0:c]], axis=-1)) for q in units}
    arw = {q: _dot(a_rb[q], wu[q]) for q in units}
    r_hat = {q: r_t[q] - arw[q][:, 0:HEAD_DIM] for q in units}
    y_loc = {q: av[q][c:2 * c] - arw[q][:, HEAD_DIM:] for q in units}
    g_mat = {q: _dot_tn(wu[q][:, 0:HEAD_DIM], b_t[q]) for q in units}
    h_loc = {q: _dot_tn(jnp.concatenate([vv[q], wu[q][:, HEAD_DIM:]], axis=0),
                        jnp.concatenate([k_t[q], -b_t[q]], axis=0)) for q in units}

    y_heads = []
    for h in range(N_HEADS):
        s_cur = st[h]
        ys = []
        for gi in range(g_chunks):
            q = (gi, h)
            ys.append(_dot_nt(r_hat[q], s_cur) + y_loc[q])
            decay_end = e_in[(gi + 1) * c - 1:(gi + 1) * c, h * HEAD_DIM:(h + 1) * HEAD_DIM]
            s_cur = (s_cur - _dot(s_cur, g_mat[q]) + h_loc[q]) * decay_end
        st[h] = s_cur
        y_heads.append(jnp.concatenate(ys, axis=0) if g_chunks > 1 else ys[0])
    yy = jnp.concatenate(y_heads, axis=-1)
    inv_dim = 1.0 / HEAD_DIM
    yc = yy - head_sums(yy) * inv_dim
    yn = yc * lax.rsqrt(head_sums(yc * yc) * inv_dim + RWKV_LN_EPS)
    bonus = head_sums(r * k2 * r_k) * v
    y_ref[...] = (yn * ln_w + ln_b + bonus) * gate

    @pl.when(ci == pl.num_programs(1) - 1)
    def _():
        snew_ref[...] = st[...]


RWKV_CHUNKS_PER_STEP = 8
RWKV_INV_BASE = 16


def _rwkv(pd, shift_prev, wkv_prev, vec_params, w2, a2, g2, n_seq, seq_len):
    c = math.gcd(seq_len, RWKV_CHUNK)
    g_chunks = math.gcd(seq_len // c, RWKV_CHUNKS_PER_STEP)
    n = c * g_chunks
    nc = seq_len // n
    tok = lambda b, i: (b * nc + i, 0)
    fixed2 = lambda b, i: (0, 0)
    y, shift_new, wkv_new = pl.pallas_call(
        functools.partial(_rwkv_kernel, c=c, g_chunks=g_chunks),
        grid=(n_seq, nc),
        in_specs=[pl.BlockSpec((n, RWKV_PROJ), tok),
                  pl.BlockSpec((None, 1, RWKV_PROJ), lambda b, i: (b, 0, 0)),
                  pl.BlockSpec((None, N_HEADS, HEAD_DIM, HEAD_DIM), lambda b, i: (b, 0, 0, 0)),
                  pl.BlockSpec((SUBLANES, RWKV_PROJ), fixed2),
                  pl.BlockSpec((64, GROUP_W), fixed2), pl.BlockSpec((64, GROUP_W), fixed2),
                  pl.BlockSpec((128, GROUP_W), fixed2)],
        out_specs=[pl.BlockSpec((n, GROUP_W), tok),
                   pl.BlockSpec((None, 1, RWKV_PROJ), lambda b, i: (b, 0, 0)),
                   pl.BlockSpec((None, N_HEADS, HEAD_DIM, HEAD_DIM), lambda b, i: (b, 0, 0, 0))],
        out_shape=[jax.ShapeDtypeStruct((n_seq * seq_len, GROUP_W), F32),
                   jax.ShapeDtypeStruct((n_seq, 1, RWKV_PROJ), F32),
                   jax.ShapeDtypeStruct((n_seq, N_HEADS, HEAD_DIM, HEAD_DIM), F32)],
        scratch_shapes=[pltpu.VMEM((n + SUBLANES, RWKV_PROJ), F32), pltpu.VMEM((N_HEADS, HEAD_DIM, HEAD_DIM), F32)],
        compiler_params=_cparams(("arbitrary", "arbitrary"), 32),
        name="rwkv",
    )(pd, shift_prev, wkv_prev, vec_params, w2, a2, g2)
    return y, shift_new[:, 0, :], wkv_new


def _rope_tables(pos, unit):
    rot = unit // 4
    half = rot // 2
    inv_freq = ROPE_THETA ** (-jnp.arange(half, dtype=jnp.float32) / half)
    ang = pos.astype(jnp.float32)[:, None] * inv_freq[None, :]
    cos, sin = jnp.cos(ang), jnp.sin(ang)
    t = pos.shape[0]
    zero_h = jnp.zeros((t, half), F32)
    rest0 = jnp.zeros((t, unit - rot), F32)
    c = jnp.concatenate([cos, cos, jnp.ones((t, unit - rot), F32)], axis=-1)
    sa = jnp.concatenate([-sin, zero_h, rest0], axis=-1)
    sb = jnp.concatenate([zero_h, sin, rest0], axis=-1)
    reps = GROUP_W // unit
    return tuple(jnp.tile(a, (1, reps)) for a in (c, sa, sb))


def _rope(x, c, sa, sb, half):
    n = x.shape[-1]
    return x * c + pltpu.roll(x, n - half, 1) * sa + pltpu.roll(x, half, 1) * sb


def _diff_params(p):
    rows = [jnp.pad(p[n], (0, LANES - DIFF_DH)) for n in ('lambda_q1', 'lambda_k1', 'lambda_q2', 'lambda_k2')]
    rows.append(jnp.pad(p['subln_w'], (0, LANES - 2 * DIFF_DH)))
    return jnp.pad(jnp.stack(rows), ((0, SUBLANES - 5), (0, 0)))


def _diff_lambda(dp, lam_init):
    return (jnp.exp(jnp.sum(dp[0:1, :] * dp[1:2, :], axis=-1, keepdims=True))
            - jnp.exp(jnp.sum(dp[2:3, :] * dp[3:4, :], axis=-1, keepdims=True)) + lam_init)


ATT_TILE = 256
ATT_KEYS = 256
ATT_SUB = ATT_TILE // ATT_KEYS
VT_ROWS = HEAD_DIM + SUBLANES


def _prep_kv(k_ref, v_ref, tabs, half, kt_out_ref, vt_out_ref, k_scr, vt_scr):
    c_ref, sa_ref, sb_ref = tabs
    kr = _rope(k_ref[...], c_ref[...], sa_ref[...], sb_ref[...], half)
    kt_out_ref[...] = kr.T
    vt32 = v_ref[...].T
    vt_out_ref[...] = vt32
    vt = vt32.astype(BF16)
    ones = jnp.ones((VT_ROWS - HEAD_DIM, vt.shape[1]), BF16)
    for h in range(N_HEADS):
        k_scr[h] = kr[:, h * HEAD_DIM:(h + 1) * HEAD_DIM].astype(BF16)
        vt_scr[h] = jnp.concatenate([vt[h * HEAD_DIM:(h + 1) * HEAD_DIM, :], ones], axis=0)
    return kr


def _kv_tiles(k_scr, vt_scr, h, off):
    return k_scr[h, pl.ds(off, ATT_KEYS), :], vt_scr[h, :, pl.ds(off, ATT_KEYS)]


def _attend_step(k_scr, vt_scr, chains, off, carries, keeps):
    tiles = {h: _kv_tiles(k_scr, vt_scr, h, off) for h in {h for h, _ in chains}}
    scores = [_dot(tiles[h][0], qs) for h, qs in chains]
    stats = []
    for s, carry, keep in zip(scores, carries, keeps):
        if keep is not None:
            s = jnp.where(keep, s, NEG_BIG)
        m_new = jnp.max(s, axis=0, keepdims=True)
        alpha = None
        if carry is not None:
            m_new = jnp.maximum(carry[0], m_new)
            alpha = jnp.exp(carry[0] - m_new)
        stats.append((m_new, alpha, jnp.exp((s - m_new).astype(BF16))))
    pvs = [_dot(tiles[h][1], st[2]) for (h, _), st in zip(chains, stats)]
    return tuple((st[0], pv if st[1] is None else st[1] * carry[1] + pv)
                 for st, pv, carry in zip(stats, pvs, carries))


def _attend_result(carry):
    acc = carry[1]
    return acc[0:HEAD_DIM, :] / acc[HEAD_DIM:HEAD_DIM + 1, :]


def _own_block(k_scr, vt_scr, chains, start):
    key_i = lax.broadcasted_iota(jnp.int32, (ATT_KEYS, ATT_TILE), 0)
    qry_i = lax.broadcasted_iota(jnp.int32, (ATT_KEYS, ATT_TILE), 1)
    carries = [None] * len(chains)
    for r in range(ATT_SUB):
        keep = key_i + r * ATT_KEYS <= qry_i
        carries = _attend_step(k_scr, vt_scr, chains, start + r * ATT_KEYS, carries, [keep] * len(chains))
    return carries


def _moba_prompt_kernel(q_ref, k_ref, v_ref, c_ref, sa_ref, sb_ref, kstack_ref, vstack_ref,
                        y_ref, kt_out_ref, vt_out_ref, k_scr, vt_scr, km_scr, *, nb):
    del kstack_ref, vstack_ref
    i = pl.program_id(1)
    tq = ATT_TILE
    half = HEAD_DIM // 8
    nbp = km_scr.shape[0]

    @pl.when(i == 0)
    def _():
        kr = _prep_kv(k_ref, v_ref, (c_ref, sa_ref, sb_ref), half, kt_out_ref, vt_out_ref, k_scr, vt_scr)
        km_scr[...] = jnp.zeros_like(km_scr)
        for n in range(nb):
            km_scr[n:n + 1, :] = jnp.mean(kr[n * tq:(n + 1) * tq, :], axis=0, keepdims=True)

    start = pl.multiple_of(i * tq, tq)
    q = _rope(q_ref[...], c_ref[pl.ds(start, tq), :], sa_ref[pl.ds(start, tq), :], sb_ref[pl.ds(start, tq), :], half)
    qt = q.T
    scale = HEAD_DIM ** -0.5
    blk_i = lax.broadcasted_iota(jnp.int32, (nbp, tq), 0)
    valid = blk_i < i

    chains, sels = [], []
    for h in range(N_HEADS):
        qth = qt[h * HEAD_DIM:(h + 1) * HEAD_DIM, :]
        gate = jnp.where(valid, _dot(km_scr[:, h * HEAD_DIM:(h + 1) * HEAD_DIM], qth, HI), -jnp.inf)
        cnt = jnp.zeros((nbp, tq), F32)
        for m in range(nb):
            gm = gate[m:m + 1, :]
            cnt += jnp.where((gm > gate) | ((gm == gate) & (m < blk_i)), 1.0, 0.0)
        sels.append(jnp.where(valid & (cnt < MOBA_TOPK), 1.0, 0.0))
        chains.append((h, (qth * scale).astype(BF16)))

    def body(n, carries):
        off = pl.multiple_of(n * ATT_KEYS, ATT_KEYS)
        chosen = [jnp.sum(jnp.where(blk_i == n // ATT_SUB, sel, 0.0), axis=0, keepdims=True) > 0.5 for sel in sels]
        return _attend_step(k_scr, vt_scr, chains, off, carries, chosen)

    fin = lax.fori_loop(0, i * ATT_SUB, body, tuple(_own_block(k_scr, vt_scr, chains, start)))
    yt = jnp.concatenate([_attend_result(carry) for carry in fin], axis=0)
    y_ref[...] = yt.T


def _kv_from_stack(stack):
    depth, n_seq, _, t = stack.shape
    return jnp.transpose(stack.reshape(depth, n_seq, N_HEADS, HEAD_DIM, t), (0, 1, 4, 2, 3))


def _kv_stack_specs(layer, seq_len):
    in_spec = pl.BlockSpec(memory_space=pl.ANY)
    out_spec = pl.BlockSpec((None, None, GROUP_W, seq_len), lambda b, i: (layer, b, 0, 0))
    return in_spec, out_spec


def _moba_prompt(q, k, v, tabs, k_stack, v_stack, layer, n_seq, seq_len):
    nb = seq_len // ATT_TILE
    nbp = -(-nb // SUBLANES) * SUBLANES
    seq = lambda b, i: (b, 0)
    fixed = lambda b, i: (0, 0)
    full = pl.BlockSpec((seq_len, GROUP_W), seq)
    tab = pl.BlockSpec((seq_len, GROUP_W), fixed)
    tile = pl.BlockSpec((ATT_TILE, GROUP_W), lambda b, i: (b * nb + i, 0))
    stack_in, stack_out = _kv_stack_specs(layer, seq_len)
    stack_sd = jax.ShapeDtypeStruct(k_stack.shape, F32)
    return pl.pallas_call(
        functools.partial(_moba_prompt_kernel, nb=nb),
        grid=(n_seq, nb),
        in_specs=[tile, full, full, tab, tab, tab, stack_in, stack_in],
        out_specs=[tile, stack_out, stack_out],
        out_shape=[jax.ShapeDtypeStruct((n_seq * seq_len, GROUP_W), F32), stack_sd, stack_sd],
        input_output_aliases={6: 1, 7: 2},
        scratch_shapes=[pltpu.VMEM((N_HEADS, seq_len, HEAD_DIM), BF16), pltpu.VMEM((N_HEADS, VT_ROWS, seq_len), BF16),
                        pltpu.VMEM((nbp, GROUP_W), F32)],
        compiler_params=_cparams(("arbitrary", "arbitrary"), 56),
        name="moba_prompt",
    )(q, k, v, *tabs, k_stack, v_stack)


def _diff_prompt_kernel(q_ref, k_ref, v_ref, c_ref, sa_ref, sb_ref, dp_ref, sw_ref, kstack_ref, vstack_ref,
                        y_ref, kt_out_ref, vt_out_ref, k_scr, vt_scr, *, lam_init):
    del kstack_ref, vstack_ref
    i = pl.program_id(1)
    tq = ATT_TILE
    half = DIFF_DH // 8

    @pl.when(i == 0)
    def _():
        _prep_kv(k_ref, v_ref, (c_ref, sa_ref, sb_ref), half, kt_out_ref, vt_out_ref, k_scr, vt_scr)

    start = pl.multiple_of(i * tq, tq)
    q = _rope(q_ref[...], c_ref[pl.ds(start, tq), :], sa_ref[pl.ds(start, tq), :], sb_ref[pl.ds(start, tq), :], half)
    qt = q.T
    scale = DIFF_DH ** -0.5
    dim_i = lax.broadcasted_iota(jnp.int32, (HEAD_DIM, tq), 0)

    chains = []
    for h in range(N_HEADS):
        qth = qt[h * HEAD_DIM:(h + 1) * HEAD_DIM, :] * scale
        for j in range(2):
            chains.append((h, jnp.where((dim_i >= DIFF_DH) == (j == 1), qth, 0.0).astype(BF16)))

    def body(n, carries):
        off = pl.multiple_of(n * ATT_KEYS, ATT_KEYS)
        return _attend_step(k_scr, vt_scr, chains, off, carries, [None] * len(chains))

    fin = lax.fori_loop(0, i * ATT_SUB, body, tuple(_own_block(k_scr, vt_scr, chains, start)))
    lam = _diff_lambda(dp_ref[...], lam_init)
    outs = []
    for h in range(N_HEADS):
        o = _attend_result(fin[2 * h]) - lam * _attend_result(fin[2 * h + 1])
        o = o * lax.rsqrt(jnp.mean(o * o, axis=0, keepdims=True) + RMS_EPS) * sw_ref[...]
        outs.append(o * (1.0 - lam_init))
    y_ref[...] = jnp.concatenate(outs, axis=0).T


def _diff_prompt(q, k, v, tabs, dparams, subln_w, lam_init, k_stack, v_stack, layer, n_seq, seq_len):
    nt = seq_len // ATT_TILE
    seq = lambda b, i: (b, 0)
    fixed = lambda b, i: (0, 0)
    full = pl.BlockSpec((seq_len, GROUP_W), seq)
    tab = pl.BlockSpec((seq_len, GROUP_W), fixed)
    tile = pl.BlockSpec((ATT_TILE, GROUP_W), lambda b, i: (b * nt + i, 0))
    subln_cols = jnp.broadcast_to(subln_w[:, None], (HEAD_DIM, ATT_TILE))
    stack_in, stack_out = _kv_stack_specs(layer, seq_len)
    stack_sd = jax.ShapeDtypeStruct(k_stack.shape, F32)
    return pl.pallas_call(
        functools.partial(_diff_prompt_kernel, lam_init=float(lam_init)),
        grid=(n_seq, nt),
        in_specs=[tile, full, full, tab, tab, tab, pl.BlockSpec((SUBLANES, LANES), fixed),
                  pl.BlockSpec((HEAD_DIM, ATT_TILE), fixed), stack_in, stack_in],
        out_specs=[tile, stack_out, stack_out],
        out_shape=[jax.ShapeDtypeStruct((n_seq * seq_len, GROUP_W), F32), stack_sd, stack_sd],
        input_output_aliases={8: 1, 9: 2},
        scratch_shapes=[pltpu.VMEM((N_HEADS, seq_len, HEAD_DIM), BF16), pltpu.VMEM((N_HEADS, VT_ROWS, seq_len), BF16)],
        compiler_params=_cparams(("arbitrary", "arbitrary"), 56),
        name="diff_prompt",
    )(q, k, v, *tabs, dparams, subln_cols, k_stack, v_stack)


SAMPLE_T = 8
NQ_MOBA = N_HEADS * SAMPLE_T
NQ_DIFF = 2 * N_HEADS * SAMPLE_T


def _block_diag_queries(q, n_maps):
    nq = n_maps * N_HEADS * SAMPLE_T
    tiled = jnp.concatenate([q] * (nq // SAMPLE_T) + [jnp.zeros((LANES - nq, GROUP_W), F32)], axis=0)
    row = lax.broadcasted_iota(jnp.int32, (LANES, GROUP_W), 0)
    lane = lax.broadcasted_iota(jnp.int32, (LANES, GROUP_W), 1)
    width = HEAD_DIM // n_maps
    keep = (row < nq) & ((row // SAMPLE_T) % N_HEADS == lane // HEAD_DIM) & (row // (N_HEADS * SAMPLE_T) == (lane % HEAD_DIM) // width)
    return jnp.where(keep, tiled, 0.0)


def _own_partials(k_new, v_new, qblk, scale, nq):
    pad = jnp.zeros((LANES - SAMPLE_T, GROUP_W), F32)
    kp = jnp.concatenate([k_new, pad], axis=0)
    vp = jnp.concatenate([v_new, pad], axis=0)
    s = _dot_nt(qblk[0:nq, :], kp) * scale
    key = lax.broadcasted_iota(jnp.int32, (nq, LANES), 1)
    qry = lax.broadcasted_iota(jnp.int32, (nq, LANES), 0) % SAMPLE_T
    s = jnp.where(key <= qry, s, NEG_BIG)
    m = s.max(axis=-1, keepdims=True)
    p = jnp.exp(s - m)
    return m, p.sum(axis=-1, keepdims=True), _dot(p, vp)


def _merge_cols(m_scr, l_scr, o_scr, sel, own, nblk, nq):
    m_own, l_own, o_own = own
    m_all = m_scr[0:nq, :]
    m_tot = jnp.maximum(jnp.max(jnp.where(sel, m_all, NEG_BIG), axis=-1, keepdims=True), m_own)
    w = jnp.where(sel, jnp.exp(m_all - m_tot), 0.0)
    e_own = jnp.exp(m_own - m_tot)
    den = jnp.sum(w * l_scr[0:nq, :], axis=-1, keepdims=True) + e_own * l_own
    num = e_own * o_own
    for n in range(nblk):
        num += w[:, n:n + 1] * o_scr[n]
    return num / den


def _sample_attn_kernel(pt_ref, mq_ref, mk_ref, mv_ref, dq_ref, dk_ref, dv_ref, *rest, nblk, per_blk, bps, lam_init):
    per_step = per_blk * bps
    n_pages = 4 * per_step
    page_refs = rest[:n_pages]
    (mc_ref, msa_ref, msb_ref, dc_ref, dsa_ref, dsb_ref, dp_ref,
     ym_ref, yd_ref, mkout_ref, dkout_ref,
     qm_scr, qd_scr, g_scr, mm_scr, lm_scr, om_scr, md_scr, ld_scr, od_scr) = rest[n_pages:]
    n = pl.program_id(1)
    m_half = HEAD_DIM // 8
    d_half = DIFF_DH // 8
    m_scale = HEAD_DIM ** -0.5
    d_scale = DIFF_DH ** -0.5

    @pl.when(n == 0)
    def _():
        qm = _rope(mq_ref[...], mc_ref[...], msa_ref[...], msb_ref[...], m_half)
        qm_scr[...] = _block_diag_queries(qm, 1)
        qd = _rope(dq_ref[...], dc_ref[...], dsa_ref[...], dsb_ref[...], d_half)
        qd_scr[...] = _block_diag_queries(qd, 2)
        for scr in (g_scr, mm_scr, lm_scr, md_scr, ld_scr):
            scr[...] = jnp.zeros_like(scr)

    def pages(cache_idx, i):
        base = cache_idx * per_step + i * per_blk
        return [page_refs[base + j][...].reshape(GROUP_W, LANES) for j in range(per_blk)]

    lane_m = lax.broadcasted_iota(jnp.int32, (NQ_MOBA, LANES), 1)
    lane_d = lax.broadcasted_iota(jnp.int32, (NQ_DIFF, LANES), 1)
    qm_blk = qm_scr[...]
    qd_blk = qd_scr[...]
    mm, lm = mm_scr[0:NQ_MOBA, :], lm_scr[0:NQ_MOBA, :]
    md, ld = md_scr[0:NQ_DIFF, :], ld_scr[0:NQ_DIFF, :]
    km_all = g_scr[...]
    lane_k = lax.broadcasted_iota(jnp.int32, (GROUP_W, LANES), 1)
    units = [(i, kind) for i in range(bps) for kind in (0, 1)]
    qbs = (qm_blk[0:NQ_MOBA, :].astype(BF16), qd_blk[0:NQ_DIFF, :].astype(BF16))
    scales = (m_scale, d_scale)
    kts = {(i, kind): pages(2 * kind, i) for i, kind in units}
    vts = {(i, kind): pages(2 * kind + 1, i) for i, kind in units}
    scores = {u: [_dot(qbs[u[1]], kt) * scales[u[1]] for kt in kts[u]] for u in units}
    k_means = [sum(kts[(i, 0)]).sum(axis=-1, keepdims=True) * (1.0 / MOBA_BLOCK) for i in range(bps)]
    probs, stats = {}, {}
    for u in units:
        m = scores[u][0].max(axis=-1, keepdims=True)
        for s in scores[u][1:]:
            m = jnp.maximum(m, s.max(axis=-1, keepdims=True))
        probs[u] = [jnp.exp(s - m) for s in scores[u]]
        stats[u] = (m, sum(p.sum(axis=-1, keepdims=True) for p in probs[u]))
    outs = {u: sum(_dot_nt(p, vt) for p, vt in zip(probs[u], vts[u])) for u in units}
    for i in range(bps):
        blk = n * bps + i
        mm = jnp.where(lane_m == blk, stats[(i, 0)][0], mm)
        lm = jnp.where(lane_m == blk, stats[(i, 0)][1], lm)
        km_all = jnp.where(lane_k == blk, k_means[i], km_all)
        md = jnp.where(lane_d == blk, stats[(i, 1)][0], md)
        ld = jnp.where(lane_d == blk, stats[(i, 1)][1], ld)
        om_scr[blk] = outs[(i, 0)]
        od_scr[blk] = outs[(i, 1)]
    mm_scr[0:NQ_MOBA, :], lm_scr[0:NQ_MOBA, :] = mm, lm
    md_scr[0:NQ_DIFF, :], ld_scr[0:NQ_DIFF, :] = md, ld
    g_scr[...] = km_all

    @pl.when(n == nblk // bps - 1)
    def _():
        k_new = _rope(mk_ref[...], mc_ref[...], msa_ref[...], msb_ref[...], m_half)
        mkout_ref[...] = k_new
        own = _own_partials(k_new, mv_ref[...], qm_scr[...], m_scale, NQ_MOBA)
        gate = jnp.where(lane_m < nblk, _dot(qm_scr[0:NQ_MOBA, :], g_scr[...], HI), -jnp.inf)
        cnt = jnp.zeros((NQ_MOBA, LANES), F32)
        for j in range(nblk):
            gj = gate[:, j:j + 1]
            cnt += jnp.where((gj > gate) | ((gj == gate) & (j < lane_m)), 1.0, 0.0)
        sel = (lane_m < nblk) & (cnt < MOBA_TOPK)
        rows_m = _merge_cols(mm_scr, lm_scr, om_scr, sel, own, nblk, NQ_MOBA)
        ym_ref[...] = jnp.concatenate(
            [rows_m[h * SAMPLE_T:(h + 1) * SAMPLE_T, h * HEAD_DIM:(h + 1) * HEAD_DIM] for h in range(N_HEADS)], axis=-1)

        k_new = _rope(dk_ref[...], dc_ref[...], dsa_ref[...], dsb_ref[...], d_half)
        dkout_ref[...] = k_new
        own = _own_partials(k_new, dv_ref[...], qd_scr[...], d_scale, NQ_DIFF)
        rows_d = _merge_cols(md_scr, ld_scr, od_scr, lane_d < nblk, own, nblk, NQ_DIFF)
        dp = dp_ref[...]
        lam = _diff_lambda(dp, lam_init)
        subln = dp[4:5, 0:2 * DIFF_DH]
        outs = []
        for h in range(N_HEADS):
            o1 = rows_d[h * SAMPLE_T:(h + 1) * SAMPLE_T, h * HEAD_DIM:(h + 1) * HEAD_DIM]
            o2 = rows_d[NQ_MOBA + h * SAMPLE_T:NQ_MOBA + (h + 1) * SAMPLE_T, h * HEAD_DIM:(h + 1) * HEAD_DIM]
            outs.append(_rms(o1 - lam * o2, subln) * (1.0 - lam_init))
        yd_ref[...] = jnp.concatenate(outs, axis=-1)


SAMPLE_BLOCKS_PER_STEP = 8


def _sample_attn(mq, mk, mv, dq, dk, dv, caches_t, page_table, tabs_m, tabs_d, dparams, lam_init, layer, n_seq):
    n_pages = page_table.shape[1]
    page = caches_t[0].shape[-1]
    per_blk = MOBA_BLOCK // page
    nblk = n_pages // per_blk
    assert page == LANES and nblk <= LANES
    bps = math.gcd(nblk, SAMPLE_BLOCKS_PER_STEP)
    per_step = per_blk * bps
    new = pl.BlockSpec((SAMPLE_T, GROUP_W), lambda b, n, pt: (b, 0))
    tab = pl.BlockSpec((SAMPLE_T, GROUP_W), lambda b, n, pt: (0, 0))

    def page_spec(j):
        return pl.BlockSpec((None, None, N_HEADS, HEAD_DIM, page),
                            lambda b, n, pt: (layer, pt[b, per_step * n + j], 0, 0, 0))

    cache_specs, cache_args = [], []
    for pool in caches_t:
        for j in range(per_step):
            cache_specs.append(page_spec(j))
            cache_args.append(pool)
    out_sd = jax.ShapeDtypeStruct((n_seq * SAMPLE_T, GROUP_W), F32)
    stat = pltpu.VMEM((LANES, LANES), F32)
    grid_spec = pltpu.PrefetchScalarGridSpec(
        num_scalar_prefetch=1,
        grid=(n_seq, nblk // bps),
        in_specs=[new] * 6 + cache_specs + [tab] * 6 + [pl.BlockSpec((SUBLANES, LANES), lambda b, n, pt: (0, 0))],
        out_specs=[new] * 4,
        scratch_shapes=[pltpu.VMEM((LANES, GROUP_W), F32), pltpu.VMEM((LANES, GROUP_W), F32),
                        pltpu.VMEM((GROUP_W, LANES), F32), stat, stat, pltpu.VMEM((nblk, NQ_MOBA, GROUP_W), F32),
                        stat, stat, pltpu.VMEM((nblk, NQ_DIFF, GROUP_W), F32)],
    )
    return pl.pallas_call(
        functools.partial(_sample_attn_kernel, nblk=nblk, per_blk=per_blk, bps=bps, lam_init=float(lam_init)),
        grid_spec=grid_spec,
        out_shape=[out_sd] * 4,
        compiler_params=_cparams(("arbitrary", "arbitrary"), 32),
        name="sample_attn",
    )(page_table, mq, mk, mv, dq, dk, dv, *cache_args, *tabs_m, *tabs_d, dparams)


def _reorder_w_in(w_in):
    o_a = GROUP_W + SSM_XBC + N_HEADS
    xbc = w_in[..., GROUP_W:GROUP_W + SSM_XBC]
    z = w_in[..., 0:GROUP_W]
    dt = jnp.pad(w_in[..., GROUP_W + SSM_XBC:o_a], ((0, 0), (0, 0), (0, LANES - N_HEADS)))
    rest = w_in[..., o_a:]
    return jnp.concatenate([xbc, z, rest, dt], axis=-1)


def kernel(x_prompt, x_sample, c_prompt, c_sample, cache_moba_k, cache_moba_v, cache_diff_k, cache_diff_v, page_table, state_ssm, state_conv, state_wkv, state_shift, w_ada, b_ada, norm_mix_pre, norm_mix_post, norm_ffn_pre, norm_ffn_post, w_in, w_out, conv_w, conv_b, dt_bias, a_log, d_skip, ssm_norm_w, lambda_q1, lambda_k1, lambda_q2, lambda_k2, subln_w, mu_shift, w0, w2, a0, a2, g2, k_k, k_a, r_k, ln_x_w, ln_x_b, w_gate, w_up, w_down):
    nbp, tp, d = x_prompt.shape
    nbs, ts, _ = x_sample.shape
    depth = w_in.shape[0]
    assert ts == SAMPLE_T and d == D_MODEL and tp % MOBA_BLOCK == 0
    past_len = page_table.shape[1] * cache_moba_k.shape[2]
    assert past_len % MOBA_BLOCK == 0

    w_in_b = _reorder_w_in(w_in).astype(BF16)
    w_out_b, w_gate_b, w_up_b, w_down_b = (w.astype(BF16) for w in (w_out, w_gate, w_up, w_down))
    mods = _ada_mod(jnp.concatenate([c_prompt, c_sample], axis=0), w_ada, b_ada)
    norm_pre = norm_mix_pre.reshape(depth, 1, d)
    norms3 = jnp.stack([norm_mix_post, norm_ffn_pre, norm_ffn_post], axis=1).reshape(depth, 3, 1, d)
    pos_p = jnp.arange(tp)
    pos_s = past_len + jnp.arange(ts)
    tabs_mp, tabs_dp = _rope_tables(pos_p, HEAD_DIM), _rope_tables(pos_p, DIFF_DH)
    tabs_ms, tabs_ds = _rope_tables(pos_s, HEAD_DIM), _rope_tables(pos_s, DIFF_DH)
    caches = [jnp.transpose(c, (0, 1, 3, 4, 2)) for c in (cache_moba_k, cache_moba_v, cache_diff_k, cache_diff_v)]

    xp = x_prompt.reshape(nbp * tp, d)
    xs = x_sample.reshape(nbs * ts, d)
    zeros_conv = jnp.zeros((nbp, SUBLANES, SSM_XBC), F32)
    zeros_ssm = jnp.zeros((nbp, N_HEADS, HEAD_DIM, SSM_STATE), F32)
    zeros_shift = jnp.zeros((nbp, 1, RWKV_PROJ), F32)
    zeros_wkv = jnp.zeros((nbp, N_HEADS, HEAD_DIM, HEAD_DIM), F32)
    kv_p = [jnp.zeros((depth, nbp, GROUP_W, tp), F32) for _ in range(4)]
    new_p, new_s = [], []
    for l in range(depth):
        lam_init = 0.8 - 0.6 * math.exp(-0.3 * l)
        head_params = _ssd_head_params(dt_bias[l], a_log[l], d_skip[l])
        vec_params = _rwkv_vec_params(dict(mu_shift=mu_shift[l], w0=w0[l], a0=a0[l], k_k=k_k[l], k_a=k_a[l],
                                           r_k=r_k[l], ln_x_w=ln_x_w[l], ln_x_b=ln_x_b[l]))
        dparams = _diff_params(dict(lambda_q1=lambda_q1[l], lambda_k1=lambda_k1[l], lambda_q2=lambda_q2[l],
                                    lambda_k2=lambda_k2[l], subln_w=subln_w[l]))
        ssd_w = (conv_w[l], conv_b[l][None], head_params, ssm_norm_w[l][None])
        rwkv_w = (vec_params, w2[l], a2[l], g2[l])
        dense_w = (norms3, w_out_b, w_gate_b, w_up_b, w_down_b)

        mod_p = mods[l, :nbp].reshape(nbp, N_MOD, 1, d)
        pr = _in_proj(xp, mod_p, norm_pre, w_in_b, l, ROW_TILE)
        ya, conv_p, ssm_p = _ssd(pr['xbc'], pr['z'], pr['dt'], zeros_conv, zeros_ssm, *ssd_w, nbp, tp)
        yb, kv_p[0], kv_p[1] = _moba_prompt(pr['mq'], pr['mk'], pr['mv'], tabs_mp, kv_p[0], kv_p[1], l, nbp, tp)
        yc, kv_p[2], kv_p[3] = _diff_prompt(pr['dq'], pr['dk'], pr['dv'], tabs_dp, dparams, subln_w[l], lam_init,
                                            kv_p[2], kv_p[3], l, nbp, tp)
        yd, shift_p, wkv_p = _rwkv(pr['pd'], zeros_shift, zeros_wkv, *rwkv_w, nbp, tp)
        xp = _post(xp, (ya, yb, yc, yd), mod_p, *dense_w, l, ROW_TILE, FF_TILE)
        new_p.append((ssm_p, conv_p, wkv_p, shift_p))

        mod_s = jnp.repeat(mods[l, nbp:].reshape(nbs, N_MOD, d), ts, axis=0).transpose(1, 0, 2)[None]
        sr = _in_proj(xs, mod_s, norm_pre, w_in_b, l, nbs * ts)
        conv_prev8 = jnp.pad(state_conv[l], ((0, 0), (SUBLANES - (SSM_CONV - 1), 0), (0, 0)))
        ya, conv_s, ssm_s = _ssd(sr['xbc'], sr['z'], sr['dt'], conv_prev8, state_ssm[l], *ssd_w, nbs, ts)
        yb, yc, mk_s, dk_s = _sample_attn(sr['mq'], sr['mk'], sr['mv'], sr['dq'], sr['dk'], sr['dv'], caches,
                                          page_table, tabs_ms, tabs_ds, dparams, lam_init, l, nbs)
        yd, shift_s, wkv_s = _rwkv(sr['pd'], state_shift[l][:, None, :], state_wkv[l], *rwkv_w, nbs, ts)
        xs = _post(xs, (ya, yb, yc, yd), mod_s, *dense_w, l, nbs * ts, FF_TILE)
        new_s.append((mk_s, sr['mv'], dk_s, sr['dv'], ssm_s, conv_s, wkv_s, shift_s))

    def stack(states, i):
        return jnp.stack([s[i] for s in states])

    kv_s_shape = (depth, nbs, ts, N_HEADS, HEAD_DIM)
    out_p = tuple(_kv_from_stack(a) for a in kv_p) + tuple(stack(new_p, i) for i in range(4))
    out_s = tuple(stack(new_s, i).reshape(kv_s_shape) for i in range(4)) + tuple(stack(new_s, i) for i in range(4, 8))
    return (xp.reshape(nbp, tp, d), xs.reshape(nbs, ts, d)) + out_p + out_s
```
